```python
import math
import jax
import jax.numpy as jnp
from jax import lax
import numpy as np

D_MODEL = 1024
BATCH = 2
SEQ = 16384
DEPTH = 2

POOL_WINDOWS = (2, 4, 8, 16)
POOL_GROUP_DIM = D_MODEL // 8
POOL_WIDTH = POOL_GROUP_DIM * len(POOL_WINDOWS)
ATTN_HEADS = 8
ATTN_HEAD_DIM = 64
ATTN_WIDTH = ATTN_HEADS * ATTN_HEAD_DIM
DILATED_BRANCHES = ((128, 1), (512, 4), (2048, 16))
ATTN_BLOCK = 128
EVEN_IN_WIDTH = POOL_WIDTH + 3 * ATTN_WIDTH
MIX_WIDTH = POOL_WIDTH + ATTN_WIDTH

SSM_EXPAND = 2
SSM_INNER = SSM_EXPAND * D_MODEL
SSM_HEAD_DIM = 64
SSM_HEADS = SSM_INNER // SSM_HEAD_DIM
SSM_GROUPS = 8
SSM_STATE = 128
SSM_CONV = 4
SSM_CHUNK = 128
SSM_CONV_DIM = SSM_INNER + 2 * SSM_GROUPS * SSM_STATE
SSM_IN_WIDTH = SSM_INNER + SSM_CONV_DIM + SSM_HEADS

PEER_HEADS = 8
PEER_N_KEYS = 128
PEER_N_EXPERTS = PEER_N_KEYS * PEER_N_KEYS
PEER_KEY_DIM = 256
PEER_HALF = PEER_KEY_DIM // 2
PEER_TOPK = 16
PEER_TOKEN_BLOCK = 128
PEER_V_SCALE = PEER_HEADS ** -0.5

N_EVEN = (DEPTH + 1) // 2
N_ODD = DEPTH // 2
EPS = 1e-6

kernel_name = 'hybrid_pool_dilattn_ssd_peer'


def rmsnorm(x, g):
    xf = x.astype(jnp.float32)
    y = xf * lax.rsqrt(jnp.mean(xf * xf, axis=-1, keepdims=True) + EPS)
    return (y * g.astype(jnp.float32)).astype(x.dtype)


def alibi_slopes(n):
    return 2.0 ** (-8.0 * jnp.arange(1, n + 1, dtype=jnp.float32) / n)


def causal_pool_mixer(a, pool_w, pool_scale):
    S = a.shape[1]
    af = a.astype(jnp.float32)
    csum = jnp.cumsum(af, axis=1)
    count = jnp.arange(1, S + 1, dtype=jnp.float32)[None, :, None]
    outs = []
    for i, w in enumerate(POOL_WINDOWS):
        sl = slice(i * POOL_GROUP_DIM, (i + 1) * POOL_GROUP_DIM)
        c = csum[..., sl]
        c_lag = jnp.pad(c, ((0, 0), (w, 0), (0, 0)))[:, :S]
        mean = (c - c_lag) / jnp.minimum(count, float(w))
        mixed = (mean - af[..., sl]).astype(a.dtype)
        outs.append(jnp.einsum('bsc,cd->bsd', mixed, pool_w[i]))
    return jnp.concatenate(outs, axis=-1) * pool_scale


def dilated_branch(q, k, v, window, dilation, slopes):
    Bsz, S, H, E = q.shape
    n_sub = window // dilation
    L = S // dilation
    nb = -(-L // ATTN_BLOCK)
    Lp = nb * ATTN_BLOCK

    def to_sub(t):
        t = t.reshape(Bsz, L, dilation, H, E).transpose(0, 2, 1, 3, 4)
        t = jnp.pad(t, ((0, 0), (0, 0), (0, Lp - L), (0, 0), (0, 0)))
        return t.reshape(Bsz, dilation, nb, ATTN_BLOCK, H, E)

    def with_prev(t):
        prev = jnp.pad(t, ((0, 0), (0, 0), (1, 0), (0, 0), (0, 0), (0, 0)))[:, :, :nb]
        return jnp.concatenate([prev, t], axis=3)

    qb = to_sub(q)
    kk = with_prev(to_sub(k))
    vv = with_prev(to_sub(v))
    s = jnp.einsum('brnqhe,brnkhe->brnhqk', qb, kk, preferred_element_type=jnp.float32)
    qi = jnp.arange(ATTN_BLOCK)[:, None]
    kj = jnp.arange(2 * ATTN_BLOCK)[None, :]
    dist = qi + ATTN_BLOCK - kj
    in_band = (dist >= 0) & (dist <= n_sub)
    has_prev = (jnp.arange(nb) > 0)[:, None, None] | (kj >= ATTN_BLOCK)[None]
    valid = in_band[None] & has_prev
    bias = -slopes[:, None, None] * (dist * dilation).astype(jnp.float32)[None]
    s = s * (ATTN_HEAD_DIM ** -0.5) + bias
    s = jnp.where(valid[:, None], s, -jnp.inf)
    m = jnp.max(s, axis=-1, keepdims=True)
    p = jnp.exp(s - m)
    l = jnp.sum(p, axis=-1, keepdims=True)
    o = jnp.einsum('brnhqk,brnkhe->brnhqe', p, vv.astype(jnp.float32)) / l
    lse = (m + jnp.log(l))[..., 0]
    o = o.transpose(0, 1, 2, 4, 3, 5).reshape(Bsz, dilation, Lp, H, E)[:, :, :L]
    o = o.transpose(0, 2, 1, 3, 4).reshape(Bsz, S, H, E)
    lse = lse.transpose(0, 1, 2, 4, 3).reshape(Bsz, dilation, Lp, H)[:, :, :L]
    lse = lse.transpose(0, 2, 1, 3).reshape(Bsz, S, H)
    return o, lse


def dilated_attention(qkv, q_gain, k_gain):
    Bsz, S, _ = qkv.shape
    q, k, v = jnp.split(qkv, 3, axis=-1)
    q = rmsnorm(q.reshape(Bsz, S, ATTN_HEADS, ATTN_HEAD_DIM), q_gain)
    k = rmsnorm(k.reshape(Bsz, S, ATTN_HEADS, ATTN_HEAD_DIM), k_gain)
    v = v.reshape(Bsz, S, ATTN_HEADS, ATTN_HEAD_DIM)
    slopes = alibi_slopes(ATTN_HEADS)
    outs, lses = [], []
    for window, dilation in DILATED_BRANCHES:
        o, lse = dilated_branch(q, k, v, window, dilation, slopes)
        outs.append(o)
        lses.append(lse)
    wts = jax.nn.softmax(jnp.stack(lses, axis=0), axis=0)
    o = wts[0][..., None] * outs[0] + wts[1][..., None] * outs[1] + wts[2][..., None] * outs[2]
    return o.reshape(Bsz, S, ATTN_WIDTH).astype(qkv.dtype)


def pool_attention_mixer(h, w_in, pool_w, pool_scale, q_gain, k_gain, w_out):
    proj = jnp.einsum('bsd,de->bse', h, w_in)
    a = causal_pool_mixer(proj[..., :POOL_WIDTH], pool_w, pool_scale).astype(h.dtype)
    o = dilated_attention(proj[..., POOL_WIDTH:], q_gain, k_gain)
    return jnp.einsum('bse,ed->bsd', jnp.concatenate([a, o], axis=-1), w_out)


def ssd_chunked_scan(x, dt, A, bm, cm):
    Bsz, S, H, P = x.shape
    G, N = bm.shape[2], bm.shape[3]
    R = H // G
    Q = SSM_CHUNK
    nc = S // Q
    xdt = (x.astype(jnp.float32) * dt[..., None]).reshape(Bsz, nc, Q, G, R, P)
    da = (dt * A).reshape(Bsz, nc, Q, G, R)
    bmc = bm.astype(jnp.float32).reshape(Bsz, nc, Q, G, N)
    cmc = cm.astype(jnp.float32).reshape(Bsz, nc, Q, G, N)
    xs = (jnp.moveaxis(xdt, 1, 0), jnp.moveaxis(da, 1, 0), jnp.moveaxis(bmc, 1, 0), jnp.moveaxis(cmc, 1, 0))
    causal = jnp.tril(jnp.ones((Q, Q), dtype=bool))[None, :, :, None, None]

    def step(state, inp):
        xdt_c, da_c, b_c, c_c = inp
        acum = jnp.cumsum(da_c, axis=1)
        seg = acum[:, :, None] - acum[:, None, :]
        decay = jnp.exp(jnp.where(causal, seg, -jnp.inf))
        cb = jnp.einsum('blgn,bsgn->blsg', c_c, b_c)
        y_diag = jnp.einsum('blsgr,bsgrp->blgrp', cb[..., None] * decay, xdt_c)
        y_off = jnp.einsum('blgn,bgrpn->blgrp', c_c, state) * jnp.exp(acum)[..., None]
        last = acum[:, -1]
        w_state = jnp.exp(last[:, None] - acum)
        new_state = state * jnp.exp(last)[..., None, None] + jnp.einsum(
            'bsgn,bsgrp->bgrpn', b_c, xdt_c * w_state[..., None])
        return new_state, y_diag + y_off

    init = jnp.zeros((Bsz, G, R, P, N), jnp.float32)
    _, ys = lax.scan(step, init, xs)
    return jnp.moveaxis(ys, 0, 1).reshape(Bsz, S, H, P)


def ssd_mixer(h, w_in, conv_w, conv_b, dt_bias, a_log, d_skip, gate_gain, w_out):
    Bsz, S, _ = h.shape
    proj = jnp.einsum('bsd,de->bse', h, w_in)
    z = proj[..., :SSM_INNER]
    xbc = proj[..., SSM_INNER:SSM_INNER + SSM_CONV_DIM]
    dt_raw = proj[..., SSM_INNER + SSM_CONV_DIM:]
    xbc = lax.conv_general_dilated(
        xbc, conv_w[:, None, :], window_strides=(1,), padding=[(SSM_CONV - 1, 0)],
        dimension_numbers=('NWC', 'WIO', 'NWC'), feature_group_count=SSM_CONV_DIM) + conv_b
    xbc = jax.nn.silu(xbc)
    xh = xbc[..., :SSM_INNER].reshape(Bsz, S, SSM_HEADS, SSM_HEAD_DIM)
    bm = xbc[..., SSM_INNER:SSM_INNER + SSM_GROUPS * SSM_STATE].reshape(Bsz, S, SSM_GROUPS, SSM_STATE)
    cm = xbc[..., SSM_INNER + SSM_GROUPS * SSM_STATE:].reshape(Bsz, S, SSM_GROUPS, SSM_STATE)
    dt = jax.nn.softplus(dt_raw.astype(jnp.float32) + dt_bias.astype(jnp.float32))
    A = -jnp.exp(a_log.astype(jnp.float32))
    y = ssd_chunked_scan(xh, dt, A, bm, cm)
    y = y + d_skip.astype(jnp.float32)[:, None] * xh.astype(jnp.float32)
    y = y.reshape(Bsz, S, SSM_INNER) * jax.nn.silu(z.astype(jnp.float32))
    y = rmsnorm(y.reshape(Bsz, S, SSM_GROUPS, SSM_INNER // SSM_GROUPS),
                gate_gain.reshape(SSM_GROUPS, SSM_INNER // SSM_GROUPS))
    y = y.reshape(Bsz, S, SSM_INNER).astype(h.dtype)
    return jnp.einsum('bse,ed->bsd', y, w_out)


def peer_ffn(h, w_q, sub_keys, expert_u, expert_v):
    Bsz, S, D = h.shape
    n = PEER_TOKEN_BLOCK
    hb = h.reshape((Bsz * S) // n, n, D)
    keys = sub_keys.astype(jnp.float32)

    def block(xb):
        q = jnp.einsum('nd,de->ne', xb, w_q).astype(jnp.float32).reshape(n, PEER_HEADS, 2, PEER_HALF)
        s = jnp.einsum('nhie,ike->nhik', q, keys)
        top_s, top_i = lax.top_k(s, PEER_TOPK)
        cand_s = (top_s[:, :, 0, :, None] + top_s[:, :, 1, None, :]).reshape(n, PEER_HEADS, PEER_TOPK * PEER_TOPK)
        cand_i = (top_i[:, :, 0, :, None] * PEER_N_KEYS + top_i[:, :, 1, None, :]).reshape(n, PEER_HEADS, PEER_TOPK * PEER_TOPK)
        best_s, best_pos = lax.top_k(cand_s, PEER_TOPK)
        idx = jnp.take_along_axis(cand_i, best_pos, axis=-1)
        gate = jax.nn.softmax(best_s, axis=-1)
        u = expert_u[idx]
        act = jax.nn.gelu(jnp.einsum('nhkd,nd->nhk', u, xb, preferred_element_type=jnp.float32), approximate=False)
        coef = (gate * act).astype(xb.dtype)
        return jnp.einsum('nhk,nhkd->nd', coef, expert_v[idx])

    return lax.map(block, hb).reshape(Bsz, S, D)


def setup_inputs(seed: int = 0) -> dict:
    key = jax.random.key(seed)
    ks = jax.random.split(key, 24)
    f32 = jnp.float32

    def nrm(k, shape, scale):
        return jax.random.normal(k, shape, f32) * scale

    x = nrm(ks[0], (BATCH, SEQ, D_MODEL), 1.0)
    attn_norm = 1.0 + nrm(ks[1], (DEPTH, D_MODEL), 0.1)
    ffn_norm = 1.0 + nrm(ks[2], (DEPTH, D_MODEL), 0.1)
    even_w_in = nrm(ks[3], (N_EVEN, D_MODEL, EVEN_IN_WIDTH), D_MODEL ** -0.5)
    pool_w = nrm(ks[4], (N_EVEN, len(POOL_WINDOWS), POOL_GROUP_DIM, POOL_GROUP_DIM), POOL_GROUP_DIM ** -0.5)
    pool_scale = 1.0 + nrm(ks[5], (N_EVEN, POOL_WIDTH), 0.1)
    q_gain = 1.0 + nrm(ks[6], (N_EVEN, ATTN_HEAD_DIM), 0.1)
    k_gain = 1.0 + nrm(ks[7], (N_EVEN, ATTN_HEAD_DIM), 0.1)
    even_w_out = nrm(ks[8], (N_EVEN, MIX_WIDTH, D_MODEL), MIX_WIDTH ** -0.5)
    ssm_w_in = nrm(ks[9], (N_ODD, D_MODEL, SSM_IN_WIDTH), D_MODEL ** -0.5)
    conv_w = nrm(ks[10], (N_ODD, SSM_CONV, SSM_CONV_DIM), SSM_CONV ** -0.5)
    conv_b = nrm(ks[11], (N_ODD, SSM_CONV_DIM), 0.02)
    dt0 = jnp.exp(jax.random.uniform(ks[12], (N_ODD, SSM_HEADS), f32, math.log(1e-3), math.log(1e-1)))
    dt_bias = dt0 + jnp.log(-jnp.expm1(-dt0))
    a_log = jnp.log(jax.random.uniform(ks[13], (N_ODD, SSM_HEADS), f32, 1.0, 16.0))
    d_skip = 1.0 + nrm(ks[14], (N_ODD, SSM_HEADS), 0.1)
    gate_gain = 1.0 + nrm(ks[15], (N_ODD, SSM_INNER), 0.1)
    ssm_w_out = nrm(ks[16], (N_ODD, SSM_INNER, D_MODEL), SSM_INNER ** -0.5)
    peer_w_q = nrm(ks[17], (DEPTH, D_MODEL, PEER_HEADS * PEER_KEY_DIM), D_MODEL ** -0.5)
    peer_sub_keys = nrm(ks[18], (DEPTH, 2, PEER_N_KEYS, PEER_HALF), PEER_HALF ** -0.5)
    peer_u = nrm(ks[19], (DEPTH, PEER_N_EXPERTS, D_MODEL), D_MODEL ** -0.5)
    peer_v = nrm(ks[20], (DEPTH, PEER_N_EXPERTS, D_MODEL), PEER_V_SCALE)
    return {'x': x, 'attn_norm': attn_norm, 'ffn_norm': ffn_norm, 'even_w_in': even_w_in,
            'pool_w': pool_w, 'pool_scale': pool_scale, 'q_gain': q_gain, 'k_gain': k_gain,
            'even_w_out': even_w_out, 'ssm_w_in': ssm_w_in, 'conv_w': conv_w, 'conv_b': conv_b,
            'dt_bias': dt_bias, 'a_log': a_log, 'd_skip': d_skip, 'gate_gain': gate_gain,
            'ssm_w_out': ssm_w_out, 'peer_w_q': peer_w_q, 'peer_sub_keys': peer_sub_keys,
            'peer_u': peer_u, 'peer_v': peer_v}


def reference(x, attn_norm, ffn_norm, even_w_in, pool_w, pool_scale, q_gain, k_gain, even_w_out,
              ssm_w_in, conv_w, conv_b, dt_bias, a_log, d_skip, gate_gain, ssm_w_out,
              peer_w_q, peer_sub_keys, peer_u, peer_v):
    h = x
    for layer in range(DEPTH):
        hn = rmsnorm(h, attn_norm[layer])
        i = layer // 2
        if layer % 2 == 0:
            mix = pool_attention_mixer(hn, even_w_in[i], pool_w[i], pool_scale[i], q_gain[i], k_gain[i], even_w_out[i])
        else:
            mix = ssd_mixer(hn, ssm_w_in[i], conv_w[i], conv_b[i], dt_bias[i], a_log[i], d_skip[i], gate_gain[i], ssm_w_out[i])
        h = h + mix.astype(h.dtype)
        ffn = peer_ffn(rmsnorm(h, ffn_norm[layer]), peer_w_q[layer], peer_sub_keys[layer], peer_u[layer], peer_v[layer])
        h = h + ffn.astype(h.dtype)
    return h
```

```python
import functools
import math

import jax
import jax.numpy as jnp
from jax import lax
from jax.experimental import pallas as pl
from jax.experimental.pallas import tpu as pltpu

F32 = jnp.float32
BF16 = jnp.bfloat16
I32 = jnp.int32
EPS = 1e-6
NEG_INF = float("-inf")

LANES = 128
SUBLANES = 8
VMEM_LIMIT = 56 * 1024 * 1024

POOL_WINDOWS = (2, 4, 8, 16)
POOL_HALO = 16
ATTN_HEADS = 8
ATTN_HEAD_DIM = 64
ATTN_BLOCK = 128
DILATED_BRANCHES = ((128, 1), (512, 4), (2048, 16))
SSM_HEAD_DIM = 64
SSM_GROUPS = 8
SSM_STATE = 128
SSM_CONV = 4
SSM_CHUNK = 128
CONV_HALO = 8
PEER_HEADS = 8
PEER_N_KEYS = 128
PEER_TOPK = 16
PEER_SEL = PEER_HEADS * PEER_TOPK
ROW_TILE = 512
COL_TILE = 512


def _params(*sem):
    return pltpu.CompilerParams(dimension_semantics=sem, vmem_limit_bytes=VMEM_LIMIT)


def _dot(a, b):
    return jnp.dot(a, b, preferred_element_type=F32)


def _dot_nt(a, b):
    return lax.dot_general(a, b, (((1,), (1,)), ((), ())), preferred_element_type=F32)


def _dot_hilo(a, b_bf16):
    hi = a.astype(BF16)
    lo = (a - hi.astype(F32)).astype(BF16)
    return _dot(hi, b_bf16) + _dot(lo, b_bf16)


def _rms_rows(x, gain):
    ms = jnp.mean(x * x, axis=-1, keepdims=True)
    return x * lax.rsqrt(ms + EPS) * gain


def _sigmoid(x):
    return 1.0 / (1.0 + jnp.exp(-x))


def _window_sum(ext, w, rows):
    e = ext
    span = 1
    while span < w:
        n = e.shape[0]
        e = e[span:n, :] + e[0:n - span, :]
        span *= 2
    start = POOL_HALO + 1 - w
    return e[start:start + rows, :]


def _even_in_kernel(seq_len, h_ref, g_ref, w_ref, pw_ref, ps_ref, qkg_ref, bd_ref, o_ref,
                    xn_ref, carry_ref):
    i = pl.program_id(0)
    j = pl.program_id(1)
    tm = h_ref.shape[0]

    @pl.when(j == 0)
    def _():
        xn_ref[...] = _rms_rows(h_ref[...], g_ref[...]).astype(BF16)

    acc = _dot(xn_ref[...], w_ref[...])

    @pl.when(j == 0)
    def _():
        pos0 = (i * tm) % seq_len

        @pl.when(pos0 == 0)
        def _():
            carry_ref[...] = jnp.zeros_like(carry_ref)

        ext = jnp.concatenate([carry_ref[...], acc], axis=0)
        pos = pos0 + lax.broadcasted_iota(I32, (tm, 1), 0)
        outs = []
        for gi, w in enumerate(POOL_WINDOWS):
            sl = slice(gi * LANES, (gi + 1) * LANES)
            ws = _window_sum(ext[:, sl], w, tm)
            cnt = jnp.minimum(pos + 1, w).astype(F32)
            mixed = ws / cnt - acc[:, sl]
            outs.append(_dot(mixed.astype(BF16), pw_ref[gi]))
        o_ref[...] = jnp.concatenate(outs, axis=1) * ps_ref[...]
        carry_ref[...] = acc[tm - POOL_HALO:tm, :]

    def qk_norm(row):
        ms = _dot_hilo(acc * acc, bd_ref[...])
        o_ref[...] = acc * lax.rsqrt(ms + EPS) * qkg_ref[row:row + 1, :]

    @pl.when(j == 1)
    def _():
        qk_norm(0)

    @pl.when(j == 2)
    def _():
        qk_norm(1)

    @pl.when(j == 3)
    def _():
        o_ref[...] = acc


def _even_in(h2, gain, w_in, pool_w, pool_scale, qk_gain, bd, seq_len):
    T, D = h2.shape
    N = w_in.shape[1]
    tm, tn = ROW_TILE, COL_TILE
    return pl.pallas_call(
        functools.partial(_even_in_kernel, seq_len),
        grid=(T // tm, N // tn),
        in_specs=[
            pl.BlockSpec((tm, D), lambda i, j: (i, 0)),
            pl.BlockSpec((1, D), lambda i, j: (0, 0)),
            pl.BlockSpec((D, tn), lambda i, j: (0, j)),
            pl.BlockSpec(pool_w.shape, lambda i, j: (0, 0, 0)),
            pl.BlockSpec((1, tn), lambda i, j: (0, 0)),
            pl.BlockSpec((2, tn), lambda i, j: (0, 0)),
            pl.BlockSpec((tn, tn), lambda i, j: (0, 0)),
        ],
        out_specs=pl.BlockSpec((tm, tn), lambda i, j: (i, j)),
        out_shape=jax.ShapeDtypeStruct((T, N), F32),
        scratch_shapes=[pltpu.VMEM((tm, D), BF16), pltpu.VMEM((POOL_HALO, tn), F32)],
        compiler_params=_params("arbitrary", "arbitrary"),
    )(h2, gain, w_in, pool_w, pool_scale, qk_gain, bd)


def _attn_kernel(dilation, q_ref, kp_ref, kc_ref, vp_ref, vc_ref, o_ref, lse_ref):
    n = pl.program_id(2)
    qb = ATTN_BLOCK
    qi = lax.broadcasted_iota(I32, (qb, qb), 0)
    kj = lax.broadcasted_iota(I32, (qb, qb), 1)
    lane = lax.broadcasted_iota(I32, (qb, LANES), 1)
    valid_c = kj <= qi
    valid_p = (kj >= qi) & (n > 0)
    dist_c = ((qi - kj) * dilation).astype(F32)
    dist_p = ((qi + qb - kj) * dilation).astype(F32)
    scale = ATTN_HEAD_DIM ** -0.5
    heads_per_tile = LANES // ATTN_HEAD_DIM
    for hp in range(ATTN_HEADS // heads_per_tile):
        sl = slice(hp * LANES, (hp + 1) * LANES)
        qp = q_ref[:, sl]
        kc = kc_ref[:, sl].astype(BF16)
        kp = kp_ref[:, sl].astype(BF16)
        vc = vc_ref[:, sl].astype(BF16)
        vp = vp_ref[:, sl].astype(BF16)
        o_tile = jnp.zeros((qb, LANES), F32)
        lse_tile = jnp.zeros((qb, LANES), F32)
        for e in range(heads_per_tile):
            head = hp * heads_per_tile + e
            slope = 2.0 ** (-8.0 * (head + 1) / ATTN_HEADS)
            in_head = (lane >= e * ATTN_HEAD_DIM) & (lane < (e + 1) * ATTN_HEAD_DIM)
            qm = jnp.where(in_head, qp, 0.0).astype(BF16)
            s_c = _dot_nt(qm, kc) * scale - slope * dist_c
            s_p = _dot_nt(qm, kp) * scale - slope * dist_p
            s_c = jnp.where(valid_c, s_c, NEG_INF)
            s_p = jnp.where(valid_p, s_p, NEG_INF)
            m = jnp.maximum(jnp.max(s_c, axis=-1, keepdims=True),
                            jnp.max(s_p, axis=-1, keepdims=True))
            p_c = jnp.exp(s_c - m)
            p_p = jnp.exp(s_p - m)
            l = jnp.sum(p_c, axis=-1, keepdims=True) + jnp.sum(p_p, axis=-1, keepdims=True)
            o = (_dot(p_c.astype(BF16), vc) + _dot(p_p.astype(BF16), vp)) / l
            lse = m + jnp.log(l)
            o_tile = jnp.where(in_head, o, o_tile)
            lse_tile = jnp.where(in_head, lse, lse_tile)
        o_ref[:, sl] = o_tile
        lse_ref[:, sl] = lse_tile


def _attn_branch(proj, dilation):
    B, S, W = proj.shape
    aw = ATTN_HEADS * ATTN_HEAD_DIM
    L = S // dilation
    nb = L // ATTN_BLOCK
    cols = W // aw
    pv = proj.reshape(B, L, dilation * W)
    blk = (None, ATTN_BLOCK, aw)

    def spec(col, prev):
        if prev:
            return pl.BlockSpec(blk, lambda b, r, n: (b, jnp.maximum(n - 1, 0), r * cols + col))
        return pl.BlockSpec(blk, lambda b, r, n: (b, n, r * cols + col))

    out_spec = pl.BlockSpec(blk, lambda b, r, n: (b, n, r))
    o, lse = pl.pallas_call(
        functools.partial(_attn_kernel, dilation),
        grid=(B, dilation, nb),
        in_specs=[spec(1, False), spec(2, True), spec(2, False), spec(3, True), spec(3, False)],
        out_specs=[out_spec, out_spec],
        out_shape=[jax.ShapeDtypeStruct((B, L, dilation * aw), F32)] * 2,
        compiler_params=_params("arbitrary", "arbitrary", "arbitrary"),
    )(pv, pv, pv, pv, pv)
    return o.reshape(B, S, aw), lse.reshape(B, S, aw)


def _even_out_kernel(a_ref, o1, o2, o3, l1, l2, l3, wa_ref, wo_ref, h_ref, out_ref, mix_ref):
    j = pl.program_id(1)
    half = a_ref.shape[1]

    @pl.when(j == 0)
    def _():
        la, lb, lc = l1[...], l2[...], l3[...]
        mx = jnp.maximum(jnp.maximum(la, lb), lc)
        wa, wb, wc = jnp.exp(la - mx), jnp.exp(lb - mx), jnp.exp(lc - mx)
        den = wa + wb + wc
        o = (wa / den) * o1[...] + (wb / den) * o2[...] + (wc / den) * o3[...]
        mix_ref[:, 0:half] = a_ref[...].astype(BF16)
        mix_ref[:, half:2 * half] = o.astype(BF16)

    out_ref[...] = (h_ref[...] + _dot(mix_ref[:, 0:half], wa_ref[...])
                    + _dot(mix_ref[:, half:2 * half], wo_ref[...]))


def _even_out(proj2, os_, lses, w_out, h2):
    T, D = h2.shape
    half = w_out.shape[0] // 2
    tm, tn = ROW_TILE, COL_TILE
    row = pl.BlockSpec((tm, half), lambda i, j: (i, 0))
    return pl.pallas_call(
        _even_out_kernel,
        grid=(T // tm, D // tn),
        in_specs=[row] * 7 + [
            pl.BlockSpec((half, tn), lambda i, j: (0, j)),
            pl.BlockSpec((half, tn), lambda i, j: (1, j)),
            pl.BlockSpec((tm, tn), lambda i, j: (i, j)),
        ],
        out_specs=pl.BlockSpec((tm, tn), lambda i, j: (i, j)),
        out_shape=jax.ShapeDtypeStruct((T, D), F32),
        scratch_shapes=[pltpu.VMEM((tm, 2 * half), BF16)],
        compiler_params=_params("arbitrary", "arbitrary"),
    )(proj2, *os_, *lses, w_out, w_out, h2)


def _ssm_in_kernel(seq_len, n_plain, h_ref, g_ref, w_ref, cw_ref, cb_ref, o_ref, xn_ref, carry_ref):
    i = pl.program_id(0)
    j = pl.program_id(1)
    tm = h_ref.shape[0]

    @pl.when(j == 0)
    def _():
        xn_ref[...] = _rms_rows(h_ref[...], g_ref[...]).astype(BF16)

    acc = _dot(xn_ref[...], w_ref[...])

    @pl.when(j < n_plain)
    def _():
        o_ref[...] = acc

    @pl.when(j >= n_plain)
    def _():
        jc = j - n_plain
        pos0 = (i * tm) % seq_len

        @pl.when(pos0 == 0)
        def _():
            carry_ref[jc] = jnp.zeros(carry_ref.shape[1:], F32)

        ext = jnp.concatenate([carry_ref[jc], acc], axis=0)
        y = cb_ref[...]
        for k in range(SSM_CONV):
            off = CONV_HALO - (SSM_CONV - 1) + k
            y = y + cw_ref[k:k + 1, :] * ext[off:off + tm, :]
        o_ref[...] = y * _sigmoid(y)
        carry_ref[jc] = acc[tm - CONV_HALO:tm, :]


def _ssm_in(h2, gain, w_main, conv_w, conv_b, seq_len, inner):
    T, D = h2.shape
    N = w_main.shape[1]
    tm, tn = ROW_TILE, COL_TILE
    n_plain = inner // tn
    n_conv = N // tn - n_plain
    return pl.pallas_call(
        functools.partial(_ssm_in_kernel, seq_len, n_plain),
        grid=(T // tm, N // tn),
        in_specs=[
            pl.BlockSpec((tm, D), lambda i, j: (i, 0)),
            pl.BlockSpec((1, D), lambda i, j: (0, 0)),
            pl.BlockSpec((D, tn), lambda i, j: (0, j)),
            pl.BlockSpec((SSM_CONV, tn), lambda i, j: (0, jnp.maximum(j - n_plain, 0))),
            pl.BlockSpec((1, tn), lambda i, j: (0, jnp.maximum(j - n_plain, 0))),
        ],
        out_specs=pl.BlockSpec((tm, tn), lambda i, j: (i, j)),
        out_shape=jax.ShapeDtypeStruct((T, N), F32),
        scratch_shapes=[pltpu.VMEM((tm, D), BF16), pltpu.VMEM((n_conv, CONV_HALO, tn), F32)],
        compiler_params=_params("arbitrary", "arbitrary"),
    )(h2, gain, w_main, conv_w, conv_b)


def _dt_kernel(h_ref, g_ref, w_ref, b_ref, o_ref):
    xn = _rms_rows(h_ref[...], g_ref[...]).astype(BF16)
    raw = _dot(xn, w_ref[...]) + b_ref[...]
    o_ref[...] = jnp.maximum(raw, 0.0) + jnp.log(1.0 + jnp.exp(-jnp.abs(raw)))


def _dt_proj(h2, gain, w_dt, dt_bias):
    T, D = h2.shape
    tm = ROW_TILE
    return pl.pallas_call(
        _dt_kernel,
        grid=(T // tm,),
        in_specs=[
            pl.BlockSpec((tm, D), lambda i: (i, 0)),
            pl.BlockSpec((1, D), lambda i: (0, 0)),
            pl.BlockSpec((D, LANES), lambda i: (0, 0)),
            pl.BlockSpec((1, LANES), lambda i: (0, 0)),
        ],
        out_specs=pl.BlockSpec((tm, LANES), lambda i: (i, 0)),
        out_shape=jax.ShapeDtypeStruct((T, LANES), F32),
        compiler_params=_params("arbitrary"),
    )(h2, gain, w_dt, dt_bias)


def _ssd_kernel(z_ref, x_ref, b_ref, c_ref, dt_ref, a_ref, dsk_ref, gg_ref, ex_ref, o_ref, st_ref):
    c = pl.program_id(1)
    q = SSM_CHUNK
    gw = x_ref.shape[1] // SSM_GROUPS
    hpg = gw // SSM_HEAD_DIM

    @pl.when(c == 0)
    def _():
        st_ref[...] = jnp.zeros_like(st_ref)

    li = lax.broadcasted_iota(I32, (q, q), 0)
    si = lax.broadcasted_iota(I32, (q, q), 1)
    causal = si <= li
    tril = jnp.where(causal, 1.0, 0.0).astype(BF16)

    dt = dt_ref[...]
    da = dt * a_ref[...]
    acum = _dot_hilo_rhs(tril, da)
    acum_t = acum.T
    ex = ex_ref[...]
    dt_x = _dot_hilo(dt, ex)
    acum_x = _dot_hilo(acum, ex)
    last_x = acum_x[q - 1:q, :]
    x = x_ref[...]
    xdt = x * dt_x
    grow = jnp.exp(acum_x)
    xw = (xdt * jnp.exp(last_x - acum_x)).astype(BF16)
    sdec = jnp.exp(last_x)
    xdt_b = xdt.astype(BF16)

    for g in range(SSM_GROUPS):
        gs = slice(g * gw, (g + 1) * gw)
        bg = b_ref[:, g * SSM_STATE:(g + 1) * SSM_STATE]
        cg = c_ref[:, g * SSM_STATE:(g + 1) * SSM_STATE].astype(BF16)
        cb = _dot_nt(cg, bg.astype(BF16))
        bg_t = bg.T.astype(BF16)
        yd = []
        for r in range(hpg):
            hh = g * hpg + r
            seg = acum[:, hh:hh + 1] - acum_t[hh:hh + 1, :]
            decay = jnp.exp(jnp.where(causal, seg, NEG_INF))
            m = (cb * decay).astype(BF16)
            yd.append(_dot(m, xdt_b[:, g * gw + r * SSM_HEAD_DIM:g * gw + (r + 1) * SSM_HEAD_DIM]))
        y = jnp.concatenate(yd, axis=1)
        st = st_ref[g]
        y = y + _dot(cg, st.astype(BF16)) * grow[:, gs]
        st_ref[g] = st * sdec[:, gs] + _dot(bg_t, xw[:, gs])
        y = y + dsk_ref[:, gs] * x[:, gs]
        zg = z_ref[:, gs]
        y = y * (zg * _sigmoid(zg))
        o_ref[:, gs] = _rms_rows(y, gg_ref[:, gs])


def _dot_hilo_rhs(a_bf16, b):
    hi = b.astype(BF16)
    lo = (b - hi.astype(F32)).astype(BF16)
    return _dot(a_bf16, hi) + _dot(a_bf16, lo)


def _ssd(zxbc, dt, a_row, dskip_x, gate_gain, expand, batch, inner):
    T = zxbc.shape[0]
    q = SSM_CHUNK
    nc = T // batch // q
    bw = SSM_GROUPS * SSM_STATE
    row = lambda col: (lambda b, c: (b * nc + c, col))
    full = lambda b, c: (0, 0)
    return pl.pallas_call(
        _ssd_kernel,
        grid=(batch, nc),
        in_specs=[
            pl.BlockSpec((q, inner), row(0)),
            pl.BlockSpec((q, inner), row(1)),
            pl.BlockSpec((q, bw), row(2 * inner // bw)),
            pl.BlockSpec((q, bw), row(2 * inner // bw + 1)),
            pl.BlockSpec((q, LANES), row(0)),
            pl.BlockSpec((1, LANES), full),
            pl.BlockSpec((1, inner), full),
            pl.BlockSpec((1, inner), full),
            pl.BlockSpec((LANES, inner), full),
        ],
        out_specs=pl.BlockSpec((q, inner), row(0)),
        out_shape=jax.ShapeDtypeStruct((T, inner), F32),
        scratch_shapes=[pltpu.VMEM((SSM_GROUPS, SSM_STATE, inner // SSM_GROUPS), F32)],
        compiler_params=_params("arbitrary", "arbitrary"),
    )(zxbc, zxbc, zxbc, zxbc, dt, a_row, dskip_x, gate_gain, expand)


def _proj_res_kernel(x_ref, w_ref, h_ref, o_ref):
    o_ref[...] = h_ref[...] + _dot(x_ref[...].astype(BF16), w_ref[...])


def _proj_res(x2, w, h2):
    T, K = x2.shape
    D = w.shape[1]
    tm, tn = ROW_TILE, COL_TILE
    return pl.pallas_call(
        _proj_res_kernel,
        grid=(T // tm, D // tn),
        in_specs=[
            pl.BlockSpec((tm, K), lambda i, j: (i, 0)),
            pl.BlockSpec((K, tn), lambda i, j: (0, j)),
            pl.BlockSpec((tm, tn), lambda i, j: (i, j)),
        ],
        out_specs=pl.BlockSpec((tm, tn), lambda i, j: (i, j)),
        out_shape=jax.ShapeDtypeStruct((T, D), F32),
        compiler_params=_params("arbitrary", "arbitrary"),
    )(x2, w, h2)


def _peer_score_kernel(h_ref, g_ref, w_ref, kt_ref, s_ref, xn_out_ref, xn_ref):
    j = pl.program_id(1)

    @pl.when(j == 0)
    def _():
        xn = _rms_rows(h_ref[...], g_ref[...])
        xn_out_ref[...] = xn
        xn_ref[...] = xn.astype(BF16)

    qv = _dot(xn_ref[...], w_ref[...])
    for cidx in range(qv.shape[1] // LANES):
        sl = slice(cidx * LANES, (cidx + 1) * LANES)
        s_ref[:, sl] = _dot_hilo(qv[:, sl], kt_ref[cidx % 2])


def _peer_scores(h2, gain, w_q, keys_t):
    T, D = h2.shape
    N = w_q.shape[1]
    tm, tn = ROW_TILE, COL_TILE
    return pl.pallas_call(
        _peer_score_kernel,
        grid=(T // tm, N // tn),
        in_specs=[
            pl.BlockSpec((tm, D), lambda i, j: (i, 0)),
            pl.BlockSpec((1, D), lambda i, j: (0, 0)),
            pl.BlockSpec((D, tn), lambda i, j: (0, j)),
            pl.BlockSpec(keys_t.shape, lambda i, j: (0, 0, 0)),
        ],
        out_specs=[pl.BlockSpec((tm, tn), lambda i, j: (i, j)),
                   pl.BlockSpec((tm, D), lambda i, j: (i, 0))],
        out_shape=[jax.ShapeDtypeStruct((T, N), F32), jax.ShapeDtypeStruct((T, D), F32)],
        scratch_shapes=[pltpu.VMEM((tm, D), BF16)],
        compiler_params=_params("arbitrary", "arbitrary"),
    )(h2, gain, w_q, keys_t)


def _peer_topk_kernel(s_ref, rep_ref, idx_ref, gate_ref, bs_ref, be_ref):
    h = pl.program_id(1)
    tb = s_ref.shape[0]
    k = PEER_TOPK
    lane1 = lax.broadcasted_iota(I32, (tb, LANES), 1)
    lane2 = lax.broadcasted_iota(I32, (tb, 2 * LANES), 1)
    key_a = lane2 // k
    key_b = lane2 % k

    def stage1(a, carry):
        s0, s1, cs, ci = carry
        m0 = jnp.max(s0, axis=-1, keepdims=True)
        i0 = jnp.argmax(s0, axis=-1, keepdims=True).astype(I32)
        m1 = jnp.max(s1, axis=-1, keepdims=True)
        i1 = jnp.argmax(s1, axis=-1, keepdims=True).astype(I32)
        s0 = jnp.where(lane1 == i0, NEG_INF, s0)
        s1 = jnp.where(lane1 == i1, NEG_INF, s1)
        cs = cs + jnp.where(key_a == a, m0, 0.0) + jnp.where(key_b == a, m1, 0.0)
        ci = ci + jnp.where(key_a == a, i0 * PEER_N_KEYS, 0) + jnp.where(key_b == a, i1, 0)
        return s0, s1, cs, ci

    init = (s_ref[:, 0:LANES], s_ref[:, LANES:2 * LANES],
            jnp.zeros((tb, 2 * LANES), F32), jnp.zeros((tb, 2 * LANES), I32))
    _, _, cand_s, cand_i = lax.fori_loop(0, k, stage1, init)
    cand_f = cand_i.astype(F32)

    @pl.when(h == 0)
    def _():
        bs_ref[...] = jnp.zeros_like(bs_ref)
        be_ref[...] = jnp.zeros_like(be_ref)

    def stage2(n, carry):
        cs, bs, be = carry
        m = jnp.max(cs, axis=-1, keepdims=True)
        pos = jnp.argmax(cs, axis=-1, keepdims=True).astype(I32)
        hit = lane2 == pos
        ex = jnp.max(jnp.where(hit, cand_f, -1.0), axis=-1, keepdims=True)
        cs = jnp.where(hit, NEG_INF, cs)
        slot = lane1 == h * k + n
        bs = jnp.where(slot, m, bs)
        be = jnp.where(slot, ex, be)
        return cs, bs, be

    _, bs, be = lax.fori_loop(0, k, stage2, (cand_s, bs_ref[...], be_ref[...]))
    bs_ref[...] = bs
    be_ref[...] = be

    @pl.when(h == PEER_HEADS - 1)
    def _():
        head_of = lane1 // k
        mx = jnp.zeros((tb, LANES), F32)
        for hh in range(PEER_HEADS):
            mh = jnp.max(jnp.where(head_of == hh, bs, NEG_INF), axis=-1, keepdims=True)
            mx = jnp.where(head_of == hh, mh, mx)
        p = jnp.exp(bs - mx)
        den = jnp.zeros((tb, LANES), F32)
        for hh in range(PEER_HEADS):
            dh = jnp.sum(jnp.where(head_of == hh, p, 0.0), axis=-1, keepdims=True)
            den = jnp.where(head_of == hh, dh, den)
        gate = p / den
        gate_ref[...] = _dot_hilo(gate, rep_ref[...])
        idx_ref[...] = be.astype(I32)


def _peer_topk(scores, rep):
    T, N = scores.shape
    tb = 256
    hw = N // PEER_HEADS
    return pl.pallas_call(
        _peer_topk_kernel,
        grid=(T // tb, PEER_HEADS),
        in_specs=[pl.BlockSpec((tb, hw), lambda i, h: (i, h)),
                  pl.BlockSpec(rep.shape, lambda i, h: (0, 0))],
        out_specs=[pl.BlockSpec((tb, PEER_SEL), lambda i, h: (i, 0)),
                   pl.BlockSpec((tb, PEER_SEL * SUBLANES), lambda i, h: (i, 0))],
        out_shape=[jax.ShapeDtypeStruct((T, PEER_SEL), I32),
                   jax.ShapeDtypeStruct((T, PEER_SEL * SUBLANES), F32)],
        scratch_shapes=[pltpu.VMEM((tb, PEER_SEL), F32), pltpu.VMEM((tb, PEER_SEL), F32)],
        compiler_params=_params("arbitrary", "arbitrary"),
    )(scores, rep)


WORD_ROWS = 4


def _gather_experts(idx_ref, t, tab_ref, stage_ref):
    for kk in range(PEER_SEL):
        r0 = pl.multiple_of(idx_ref[t, kk] * WORD_ROWS, WORD_ROWS)
        stage_ref[kk * WORD_ROWS:(kk + 1) * WORD_ROWS, :] = tab_ref[pl.ds(r0, WORD_ROWS), :]
    return pltpu.bitcast(stage_ref[...], BF16)


def _diag_mask():
    n = PEER_SEL * SUBLANES
    row = lax.broadcasted_iota(I32, (SUBLANES, n), 0)
    col = lax.broadcasted_iota(I32, (SUBLANES, n), 1)
    return (col % SUBLANES == row).astype(F32)


def _peer_down_kernel(idx_ref, x_ref, gate_ref, tab_ref, sel_ref, coef_ref, stage_ref, r_ref):
    tb = x_ref.shape[0]
    mask = _diag_mask()

    def body(t, carry):
        w = _gather_experts(idx_ref, t, tab_ref, stage_ref)
        y = _dot_nt(x_ref[t].astype(BF16), w)
        r_ref[pl.ds(t, 1), :] = jnp.sum(y * mask, axis=0, keepdims=True)
        return carry

    lax.fori_loop(0, tb, body, 0)
    act = _dot_hilo(r_ref[...], sel_ref[...])
    gelu = 0.5 * act * (1.0 + lax.erf(act * (2.0 ** -0.5)))
    coef_ref[...] = gate_ref[...] * gelu


def _peer_down(idx, x3, gate_rep, tab, sel):
    T = idx.shape[0]
    tb = 64
    n = PEER_SEL * SUBLANES
    return pl.pallas_call(
        _peer_down_kernel,
        grid=(T // tb,),
        in_specs=[
            pl.BlockSpec((tb, PEER_SEL), lambda i: (i, 0), memory_space=pltpu.SMEM),
            pl.BlockSpec((tb, SUBLANES, LANES), lambda i: (i, 0, 0)),
            pl.BlockSpec((tb, n), lambda i: (i, 0)),
            pl.BlockSpec(memory_space=pltpu.VMEM),
            pl.BlockSpec(memory_space=pltpu.VMEM),
        ],
        out_specs=pl.BlockSpec((tb, n), lambda i: (i, 0)),
        out_shape=jax.ShapeDtypeStruct((T, n), F32),
        scratch_shapes=[pltpu.VMEM((PEER_SEL * WORD_ROWS, LANES), jnp.uint32),
                        pltpu.VMEM((tb, n), F32)],
        compiler_params=_params("arbitrary"),
    )(idx, x3, gate_rep, tab, sel)


def _peer_up_kernel(idx_ref, coef_ref, h_ref, tab_ref, o_ref, stage_ref):
    tb = h_ref.shape[0]
    mask = _diag_mask()

    def body(t, carry):
        w = _gather_experts(idx_ref, t, tab_ref, stage_ref)
        cm = (coef_ref[pl.ds(t, 1), :] * mask).astype(BF16)
        o_ref[t] = h_ref[t] + _dot(cm, w)
        return carry

    lax.fori_loop(0, tb, body, 0)


def _peer_up(idx, coef_rep, h3, tab):
    T = idx.shape[0]
    tb = 64
    n = PEER_SEL * SUBLANES
    return pl.pallas_call(
        _peer_up_kernel,
        grid=(T // tb,),
        in_specs=[
            pl.BlockSpec((tb, PEER_SEL), lambda i: (i, 0), memory_space=pltpu.SMEM),
            pl.BlockSpec((tb, n), lambda i: (i, 0)),
            pl.BlockSpec((tb, SUBLANES, LANES), lambda i: (i, 0, 0)),
            pl.BlockSpec(memory_space=pltpu.VMEM),
        ],
        out_specs=pl.BlockSpec((tb, SUBLANES, LANES), lambda i: (i, 0, 0)),
        out_shape=jax.ShapeDtypeStruct((T, SUBLANES, LANES), F32),
        scratch_shapes=[pltpu.VMEM((PEER_SEL * WORD_ROWS, LANES), jnp.uint32)],
        compiler_params=_params("arbitrary"),
    )(idx, coef_rep, h3, tab)


def _pack_table(tab):
    e, d = tab.shape
    t = tab.astype(BF16).reshape(e, WORD_ROWS, 2, LANES)
    t = jnp.swapaxes(t, 2, 3)
    return lax.bitcast_convert_type(t, jnp.uint32).reshape(e * WORD_ROWS, LANES)


def _peer_layer(h2, gain, w_q, sub_keys, tab_u, tab_v):
    T, D = h2.shape
    n_rep = PEER_SEL * SUBLANES
    rep_lane = jnp.arange(n_rep)
    rep = (jnp.arange(PEER_SEL)[:, None] == rep_lane[None, :] // SUBLANES).astype(BF16)
    sel = (rep_lane[:, None] // SUBLANES == rep_lane[None, :] // SUBLANES).astype(BF16)
    keys_t = jnp.swapaxes(sub_keys, 1, 2).astype(BF16)
    scores, xn = _peer_scores(h2, gain.reshape(1, D), w_q.astype(BF16), keys_t)
    idx, gate_rep = _peer_topk(scores, rep)
    x3 = xn.reshape(T, SUBLANES, LANES)
    coef_rep = _peer_down(idx, x3, gate_rep, _pack_table(tab_u), sel)
    out3 = _peer_up(idx, coef_rep, h2.reshape(T, SUBLANES, LANES), _pack_table(tab_v))
    return out3.reshape(T, D)


def _even_layer(h2, gain, w_in, pool_w, pool_scale, q_gain, k_gain, w_out, batch, seq_len):
    T, D = h2.shape
    aw = ATTN_HEADS * ATTN_HEAD_DIM
    lane = jnp.arange(aw)
    bd = ((lane[:, None] // ATTN_HEAD_DIM == lane[None, :] // ATTN_HEAD_DIM)
          .astype(F32) / ATTN_HEAD_DIM).astype(BF16)
    qk_gain = jnp.stack([jnp.tile(q_gain, ATTN_HEADS), jnp.tile(k_gain, ATTN_HEADS)])
    proj = _even_in(h2, gain, w_in.astype(BF16), pool_w.astype(BF16),
                    pool_scale.reshape(1, -1), qk_gain, bd, seq_len)
    proj3 = proj.reshape(batch, seq_len, -1)
    outs, lses = [], []
    for _, dilation in DILATED_BRANCHES:
        o, lse = _attn_branch(proj3, dilation)
        outs.append(o.reshape(T, aw))
        lses.append(lse.reshape(T, aw))
    return _even_out(proj, outs, lses, w_out.astype(BF16), h2)


def _ssm_layer(h2, gain, w_in, conv_w, conv_b, dt_bias, a_log, d_skip, gate_gain, w_out,
               batch, seq_len):
    inner = w_out.shape[0]
    n_heads = inner // SSM_HEAD_DIM
    main_w = 2 * inner + 2 * SSM_GROUPS * SSM_STATE
    zxbc = _ssm_in(h2, gain, w_in[:, :main_w].astype(BF16), conv_w, conv_b.reshape(1, -1),
                   seq_len, inner)
    w_dt = jnp.pad(w_in[:, main_w:], ((0, 0), (0, LANES - n_heads))).astype(BF16)
    b_dt = jnp.pad(dt_bias, (0, LANES - n_heads)).reshape(1, LANES)
    dt = _dt_proj(h2, gain, w_dt, b_dt)
    a_row = jnp.pad(-jnp.exp(a_log), (0, LANES - n_heads)).reshape(1, LANES)
    dskip_x = jnp.repeat(d_skip, SSM_HEAD_DIM).reshape(1, inner)
    expand = (jnp.arange(LANES)[:, None] == jnp.arange(inner)[None, :] // SSM_HEAD_DIM).astype(BF16)
    y = _ssd(zxbc, dt, a_row, dskip_x, gate_gain.reshape(1, inner), expand, batch, inner)
    return _proj_res(y, w_out.astype(BF16), h2)


def kernel(x, attn_norm, ffn_norm, even_w_in, pool_w, pool_scale, q_gain, k_gain, even_w_out, ssm_w_in, conv_w, conv_b, dt_bias, a_log, d_skip, gate_gain, ssm_w_out, peer_w_q, peer_sub_keys, peer_u, peer_v):
    B, S, D = x.shape
    T = B * S
    assert D == SUBLANES * LANES and S % ROW_TILE == 0
    assert all(S % (d * ATTN_BLOCK) == 0 for _, d in DILATED_BRANCHES)
    assert all(w // d == ATTN_BLOCK for w, d in DILATED_BRANCHES)
    h = x.reshape(T, D)
    for layer in range(attn_norm.shape[0]):
        i = layer // 2
        gain = attn_norm[layer].reshape(1, D)
        if layer % 2 == 0:
            h = _even_layer(h, gain, even_w_in[i], pool_w[i], pool_scale[i], q_gain[i], k_gain[i],
                            even_w_out[i], B, S)
        else:
            h = _ssm_layer(h, gain, ssm_w_in[i], conv_w[i], conv_b[i], dt_bias[i], a_log[i],
                           d_skip[i], gate_gain[i], ssm_w_out[i], B, S)
        h = _peer_layer(h, ffn_norm[layer], peer_w_q[layer], peer_sub_keys[layer],
                        peer_u[layer], peer_v[layer])
    return h.reshape(B, S, D)
```

```python
import functools
import math

import jax
import jax.numpy as jnp
from jax import lax
from jax.experimental import pallas as pl
from jax.experimental.pallas import tpu as pltpu

F32 = jnp.float32
BF16 = jnp.bfloat16
I32 = jnp.int32
EPS = 1e-6
NEG_INF = float("-inf")

LANES = 128
SUBLANES = 8
VMEM_LIMIT = 56 * 1024 * 1024

POOL_WINDOWS = (2, 4, 8, 16)
POOL_HALO = 16
ATTN_HEADS = 8
ATTN_HEAD_DIM = 64
ATTN_BLOCK = 128
DILATED_BRANCHES = ((128, 1), (512, 4), (2048, 16))
SSM_HEAD_DIM = 64
SSM_GROUPS = 8
SSM_STATE = 128
SSM_CONV = 4
SSM_CHUNK = 128
CONV_HALO = 8
PEER_HEADS = 8
PEER_N_KEYS = 128
PEER_TOPK = 16
PEER_SEL = PEER_HEADS * PEER_TOPK
WORD_ROWS = 4
ROW_TILE = 512
COL_TILE = 512


def _params(*sem):
    return pltpu.CompilerParams(dimension_semantics=sem, vmem_limit_bytes=VMEM_LIMIT)


def _dot(a, b):
    return jnp.dot(a, b, preferred_element_type=F32)


def _dot_nt(a, b):
    return lax.dot_general(a, b, (((1,), (1,)), ((), ())), preferred_element_type=F32)


def _dot_hilo(a, b_bf16):
    hi = a.astype(BF16)
    lo = (a - hi.astype(F32)).astype(BF16)
    return _dot(hi, b_bf16) + _dot(lo, b_bf16)


def _rms_rows(x, gain):
    ms = jnp.mean(x * x, axis=-1, keepdims=True)
    return x * lax.rsqrt(ms + EPS) * gain


def _sigmoid(x):
    return 1.0 / (1.0 + jnp.exp(-x))


def _window_sum(ext, w, rows):
    e = ext
    span = 1
    while span < w:
        n = e.shape[0]
        e = e[span:n, :] + e[0:n - span, :]
        span *= 2
    start = POOL_HALO + 1 - w
    return e[start:start + rows, :]


def _even_in_kernel(seq_len, h_ref, g_ref, w_ref, pw_ref, ps_ref, qkg_ref, bd_ref, o_ref,
                    xn_ref, carry_ref):
    i = pl.program_id(0)
    j = pl.program_id(1)
    tm = h_ref.shape[0]

    @pl.when(j == 0)
    def _():
        xn_ref[...] = _rms_rows(h_ref[...], g_ref[...]).astype(BF16)

    acc = _dot(xn_ref[...], w_ref[...])

    @pl.when(j == 0)
    def _():
        pos0 = (i * tm) % seq_len

        @pl.when(pos0 == 0)
        def _():
            carry_ref[...] = jnp.zeros_like(carry_ref)

        ext = jnp.concatenate([carry_ref[...], acc], axis=0)
        pos = pos0 + lax.broadcasted_iota(I32, (tm, 1), 0)
        outs = []
        for gi, w in enumerate(POOL_WINDOWS):
            sl = slice(gi * LANES, (gi + 1) * LANES)
            ws = _window_sum(ext[:, sl], w, tm)
            cnt = jnp.minimum(pos + 1, w).astype(F32)
            mixed = ws / cnt - acc[:, sl]
            outs.append(_dot(mixed.astype(BF16), pw_ref[gi]))
        o_ref[...] = jnp.concatenate(outs, axis=1) * ps_ref[...]
        carry_ref[...] = acc[tm - POOL_HALO:tm, :]

    def qk_norm(row):
        ms = _dot_hilo(acc * acc, bd_ref[...])
        o_ref[...] = acc * lax.rsqrt(ms + EPS) * qkg_ref[row:row + 1, :]

    @pl.when(j == 1)
    def _():
        qk_norm(0)

    @pl.when(j == 2)
    def _():
        qk_norm(1)

    @pl.when(j == 3)
    def _():
        o_ref[...] = acc


def _even_in(h2, gain, w_in, pool_w, pool_scale, qk_gain, bd, seq_len):
    T, D = h2.shape
    N = w_in.shape[1]
    tm, tn = ROW_TILE, COL_TILE
    return pl.pallas_call(
        functools.partial(_even_in_kernel, seq_len),
        grid=(T // tm, N // tn),
        in_specs=[
            pl.BlockSpec((tm, D), lambda i, j: (i, 0)),
            pl.BlockSpec((1, D), lambda i, j: (0, 0)),
            pl.BlockSpec((D, tn), lambda i, j: (0, j)),
            pl.BlockSpec(pool_w.shape, lambda i, j: (0, 0, 0)),
            pl.BlockSpec((1, tn), lambda i, j: (0, 0)),
            pl.BlockSpec((2, tn), lambda i, j: (0, 0)),
            pl.BlockSpec((tn, tn), lambda i, j: (0, 0)),
        ],
        out_specs=pl.BlockSpec((tm, tn), lambda i, j: (i, j)),
        out_shape=jax.ShapeDtypeStruct((T, N), F32),
        scratch_shapes=[pltpu.VMEM((tm, D), BF16), pltpu.VMEM((POOL_HALO, tn), F32)],
        compiler_params=_params("arbitrary", "arbitrary"),
        name="even_in",
    )(h2, gain, w_in, pool_w, pool_scale, qk_gain, bd)


def _attn_kernel(dilation, q_ref, kp_ref, kc_ref, vp_ref, vc_ref, o_ref, lse_ref):
    n = pl.program_id(2)
    qb = ATTN_BLOCK
    qi = lax.broadcasted_iota(I32, (qb, qb), 0)
    kj = lax.broadcasted_iota(I32, (qb, qb), 1)
    lane = lax.broadcasted_iota(I32, (qb, LANES), 1)
    valid_c = kj <= qi
    valid_p = (kj >= qi) & (n > 0)
    dist_c = ((qi - kj) * dilation).astype(F32)
    dist_p = ((qi + qb - kj) * dilation).astype(F32)
    scale = ATTN_HEAD_DIM ** -0.5
    heads_per_tile = LANES // ATTN_HEAD_DIM
    for hp in range(ATTN_HEADS // heads_per_tile):
        sl = slice(hp * LANES, (hp + 1) * LANES)
        qp = q_ref[:, sl]
        kc = kc_ref[:, sl].astype(BF16)
        kp = kp_ref[:, sl].astype(BF16)
        vc = vc_ref[:, sl].astype(BF16)
        vp = vp_ref[:, sl].astype(BF16)
        o_tile = jnp.zeros((qb, LANES), F32)
        lse_tile = jnp.zeros((qb, LANES), F32)
        for e in range(heads_per_tile):
            head = hp * heads_per_tile + e
            slope = 2.0 ** (-8.0 * (head + 1) / ATTN_HEADS)
            in_head = (lane >= e * ATTN_HEAD_DIM) & (lane < (e + 1) * ATTN_HEAD_DIM)
            qm = jnp.where(in_head, qp, 0.0).astype(BF16)
            s_c = _dot_nt(qm, kc) * scale - slope * dist_c
            s_p = _dot_nt(qm, kp) * scale - slope * dist_p
            s_c = jnp.where(valid_c, s_c, NEG_INF)
            s_p = jnp.where(valid_p, s_p, NEG_INF)
            m = jnp.maximum(jnp.max(s_c, axis=-1, keepdims=True),
                            jnp.max(s_p, axis=-1, keepdims=True))
            p_c = jnp.exp(s_c - m)
            p_p = jnp.exp(s_p - m)
            l = jnp.sum(p_c, axis=-1, keepdims=True) + jnp.sum(p_p, axis=-1, keepdims=True)
            o = (_dot(p_c.astype(BF16), vc) + _dot(p_p.astype(BF16), vp)) / l
            lse = m + jnp.log(l)
            o_tile = jnp.where(in_head, o, o_tile)
            lse_tile = jnp.where(in_head, lse, lse_tile)
        o_ref[:, sl] = o_tile
        lse_ref[:, sl] = lse_tile


def _attn_branch(proj, dilation):
    B, S, W = proj.shape
    aw = ATTN_HEADS * ATTN_HEAD_DIM
    L = S // dilation
    nb = L // ATTN_BLOCK
    cols = W // aw
    pv = proj.reshape(B, L, dilation * W)
    blk = (None, ATTN_BLOCK, aw)

    def spec(col, prev):
        if prev:
            return pl.BlockSpec(blk, lambda b, r, n: (b, jnp.maximum(n - 1, 0), r * cols + col))
        return pl.BlockSpec(blk, lambda b, r, n: (b, n, r * cols + col))

    out_spec = pl.BlockSpec(blk, lambda b, r, n: (b, n, r))
    o, lse = pl.pallas_call(
        functools.partial(_attn_kernel, dilation),
        grid=(B, dilation, nb),
        in_specs=[spec(1, False), spec(2, True), spec(2, False), spec(3, True), spec(3, False)],
        out_specs=[out_spec, out_spec],
        out_shape=[jax.ShapeDtypeStruct((B, L, dilation * aw), F32)] * 2,
        compiler_params=_params("arbitrary", "arbitrary", "arbitrary"),
        name="attn",
    )(pv, pv, pv, pv, pv)
    return o.reshape(B, S, aw), lse.reshape(B, S, aw)


def _even_out_kernel(a_ref, o1, o2, o3, l1, l2, l3, wa_ref, wo_ref, h_ref, out_ref, mix_ref):
    j = pl.program_id(1)
    half = a_ref.shape[1]

    @pl.when(j == 0)
    def _():
        la, lb, lc = l1[...], l2[...], l3[...]
        mx = jnp.maximum(jnp.maximum(la, lb), lc)
        wa, wb, wc = jnp.exp(la - mx), jnp.exp(lb - mx), jnp.exp(lc - mx)
        den = wa + wb + wc
        o = (wa / den) * o1[...] + (wb / den) * o2[...] + (wc / den) * o3[...]
        mix_ref[:, 0:half] = a_ref[...].astype(BF16)
        mix_ref[:, half:2 * half] = o.astype(BF16)

    out_ref[...] = (h_ref[...] + _dot(mix_ref[:, 0:half], wa_ref[...])
                    + _dot(mix_ref[:, half:2 * half], wo_ref[...]))


def _even_out(proj2, os_, lses, w_out, h2):
    T, D = h2.shape
    half = w_out.shape[0] // 2
    tm, tn = ROW_TILE, COL_TILE
    row = pl.BlockSpec((tm, half), lambda i, j: (i, 0))
    return pl.pallas_call(
        _even_out_kernel,
        grid=(T // tm, D // tn),
        in_specs=[row] * 7 + [
            pl.BlockSpec((half, tn), lambda i, j: (0, j)),
            pl.BlockSpec((half, tn), lambda i, j: (1, j)),
            pl.BlockSpec((tm, tn), lambda i, j: (i, j)),
        ],
        out_specs=pl.BlockSpec((tm, tn), lambda i, j: (i, j)),
        out_shape=jax.ShapeDtypeStruct((T, D), F32),
        scratch_shapes=[pltpu.VMEM((tm, 2 * half), BF16)],
        compiler_params=_params("arbitrary", "arbitrary"),
        name="even_out",
    )(proj2, *os_, *lses, w_out, w_out, h2)


def _ssm_in_kernel(seq_len, n_plain, h_ref, g_ref, w_ref, cw_ref, cb_ref, o_ref, xn_ref, carry_ref):
    i = pl.program_id(0)
    j = pl.program_id(1)
    tm = h_ref.shape[0]

    @pl.when(j == 0)
    def _():
        xn_ref[...] = _rms_rows(h_ref[...], g_ref[...]).astype(BF16)

    acc = _dot(xn_ref[...], w_ref[...])

    @pl.when(j < n_plain)
    def _():
        o_ref[...] = acc

    @pl.when(j >= n_plain)
    def _():
        jc = j - n_plain
        pos0 = (i * tm) % seq_len

        @pl.when(pos0 == 0)
        def _():
            carry_ref[jc] = jnp.zeros(carry_ref.shape[1:], F32)

        ext = jnp.concatenate([carry_ref[jc], acc], axis=0)
        y = cb_ref[...]
        for k in range(SSM_CONV):
            off = CONV_HALO - (SSM_CONV - 1) + k
            y = y + cw_ref[k:k + 1, :] * ext[off:off + tm, :]
        o_ref[...] = y * _sigmoid(y)
        carry_ref[jc] = acc[tm - CONV_HALO:tm, :]


def _ssm_in(h2, gain, w_main, conv_w, conv_b, seq_len, inner):
    T, D = h2.shape
    N = w_main.shape[1]
    tm, tn = ROW_TILE, COL_TILE
    n_plain = inner // tn
    n_conv = N // tn - n_plain
    return pl.pallas_call(
        functools.partial(_ssm_in_kernel, seq_len, n_plain),
        grid=(T // tm, N // tn),
        in_specs=[
            pl.BlockSpec((tm, D), lambda i, j: (i, 0)),
            pl.BlockSpec((1, D), lambda i, j: (0, 0)),
            pl.BlockSpec((D, tn), lambda i, j: (0, j)),
            pl.BlockSpec((SSM_CONV, tn), lambda i, j: (0, jnp.maximum(j - n_plain, 0))),
            pl.BlockSpec((1, tn), lambda i, j: (0, jnp.maximum(j - n_plain, 0))),
        ],
        out_specs=pl.BlockSpec((tm, tn), lambda i, j: (i, j)),
        out_shape=jax.ShapeDtypeStruct((T, N), F32),
        scratch_shapes=[pltpu.VMEM((tm, D), BF16), pltpu.VMEM((n_conv, CONV_HALO, tn), F32)],
        compiler_params=_params("arbitrary", "arbitrary"),
        name="ssm_in",
    )(h2, gain, w_main, conv_w, conv_b)


def _dt_kernel(h_ref, g_ref, w_ref, b_ref, o_ref):
    xn = _rms_rows(h_ref[...], g_ref[...]).astype(BF16)
    raw = _dot(xn, w_ref[...]) + b_ref[...]
    o_ref[...] = jnp.maximum(raw, 0.0) + jnp.log(1.0 + jnp.exp(-jnp.abs(raw)))


def _dt_proj(h2, gain, w_dt, dt_bias):
    T, D = h2.shape
    tm = ROW_TILE
    return pl.pallas_call(
        _dt_kernel,
        grid=(T // tm,),
        in_specs=[
            pl.BlockSpec((tm, D), lambda i: (i, 0)),
            pl.BlockSpec((1, D), lambda i: (0, 0)),
            pl.BlockSpec((D, LANES), lambda i: (0, 0)),
            pl.BlockSpec((1, LANES), lambda i: (0, 0)),
        ],
        out_specs=pl.BlockSpec((tm, LANES), lambda i: (i, 0)),
        out_shape=jax.ShapeDtypeStruct((T, LANES), F32),
        compiler_params=_params("arbitrary"),
        name="dt",
    )(h2, gain, w_dt, dt_bias)


def _ssd_kernel(z_ref, x_ref, b_ref, c_ref, dt_ref, a_ref, dsk_ref, gg_ref, ex_ref, o_ref, st_ref):
    c = pl.program_id(1)
    q = SSM_CHUNK
    gw = x_ref.shape[1] // SSM_GROUPS
    hpg = gw // SSM_HEAD_DIM

    @pl.when(c == 0)
    def _():
        st_ref[...] = jnp.zeros_like(st_ref)

    li = lax.broadcasted_iota(I32, (q, q), 0)
    si = lax.broadcasted_iota(I32, (q, q), 1)
    causal = si <= li
    tril = jnp.where(causal, 1.0, 0.0).astype(BF16)

    dt = dt_ref[...]
    da = dt * a_ref[...]
    acum = _dot_hilo_rhs(tril, da)
    acum_t = acum.T
    ex = ex_ref[...]
    dt_x = _dot_hilo(dt, ex)
    acum_x = _dot_hilo(acum, ex)
    last_x = acum_x[q - 1:q, :]
    x = x_ref[...]
    xdt = x * dt_x
    grow = jnp.exp(acum_x)
    xw = (xdt * jnp.exp(last_x - acum_x)).astype(BF16)
    sdec = jnp.exp(last_x)
    xdt_b = xdt.astype(BF16)

    for g in range(SSM_GROUPS):
        gs = slice(g * gw, (g + 1) * gw)
        bg = b_ref[:, g * SSM_STATE:(g + 1) * SSM_STATE]
        cg = c_ref[:, g * SSM_STATE:(g + 1) * SSM_STATE].astype(BF16)
        cb = _dot_nt(cg, bg.astype(BF16))
        bg_t = bg.T.astype(BF16)
        yd = []
        for r in range(hpg):
            hh = g * hpg + r
            seg = acum[:, hh:hh + 1] - acum_t[hh:hh + 1, :]
            decay = jnp.exp(jnp.where(causal, seg, NEG_INF))
            m = (cb * decay).astype(BF16)
            yd.append(_dot(m, xdt_b[:, g * gw + r * SSM_HEAD_DIM:g * gw + (r + 1) * SSM_HEAD_DIM]))
        y = jnp.concatenate(yd, axis=1)
        st = st_ref[g]
        y = y + _dot(cg, st.astype(BF16)) * grow[:, gs]
        st_ref[g] = st * sdec[:, gs] + _dot(bg_t, xw[:, gs])
        y = y + dsk_ref[:, gs] * x[:, gs]
        zg = z_ref[:, gs]
        y = y * (zg * _sigmoid(zg))
        o_ref[:, gs] = _rms_rows(y, gg_ref[:, gs])


def _dot_hilo_rhs(a_bf16, b):
    hi = b.astype(BF16)
    lo = (b - hi.astype(F32)).astype(BF16)
    return _dot(a_bf16, hi) + _dot(a_bf16, lo)


def _ssd(zxbc, dt, a_row, dskip_x, gate_gain, expand, batch, inner):
    T = zxbc.shape[0]
    q = SSM_CHUNK
    nc = T // batch // q
    bw = SSM_GROUPS * SSM_STATE
    row = lambda col: (lambda b, c: (b * nc + c, col))
    full = lambda b, c: (0, 0)
    return pl.pallas_call(
        _ssd_kernel,
        grid=(batch, nc),
        in_specs=[
            pl.BlockSpec((q, inner), row(0)),
            pl.BlockSpec((q, inner), row(1)),
            pl.BlockSpec((q, bw), row(2 * inner // bw)),
            pl.BlockSpec((q, bw), row(2 * inner // bw + 1)),
            pl.BlockSpec((q, LANES), row(0)),
            pl.BlockSpec((1, LANES), full),
            pl.BlockSpec((1, inner), full),
            pl.BlockSpec((1, inner), full),
            pl.BlockSpec((LANES, inner), full),
        ],
        out_specs=pl.BlockSpec((q, inner), row(0)),
        out_shape=jax.ShapeDtypeStruct((T, inner), F32),
        scratch_shapes=[pltpu.VMEM((SSM_GROUPS, SSM_STATE, inner // SSM_GROUPS), F32)],
        compiler_params=_params("arbitrary", "arbitrary"),
        name="ssd",
    )(zxbc, zxbc, zxbc, zxbc, dt, a_row, dskip_x, gate_gain, expand)


def _proj_res_kernel(x_ref, w_ref, h_ref, o_ref):
    o_ref[...] = h_ref[...] + _dot(x_ref[...].astype(BF16), w_ref[...])


def _proj_res(x2, w, h2):
    T, K = x2.shape
    D = w.shape[1]
    tm, tn = ROW_TILE, COL_TILE
    return pl.pallas_call(
        _proj_res_kernel,
        grid=(T // tm, D // tn),
        in_specs=[
            pl.BlockSpec((tm, K), lambda i, j: (i, 0)),
            pl.BlockSpec((K, tn), lambda i, j: (0, j)),
            pl.BlockSpec((tm, tn), lambda i, j: (i, j)),
        ],
        out_specs=pl.BlockSpec((tm, tn), lambda i, j: (i, j)),
        out_shape=jax.ShapeDtypeStruct((T, D), F32),
        compiler_params=_params("arbitrary", "arbitrary"),
        name="proj_res",
    )(x2, w, h2)


def _peer_score_kernel(h_ref, g_ref, w_ref, k_ref, s_ref, xn_out_ref, xn_ref):
    j = pl.program_id(1)

    @pl.when(j == 0)
    def _():
        xn = _rms_rows(h_ref[...], g_ref[...])
        xn_out_ref[...] = xn
        xn_ref[...] = xn.astype(BF16)

    qv = _dot(xn_ref[...], w_ref[...])
    for cidx in range(qv.shape[1] // LANES):
        qc = qv[:, cidx * LANES:(cidx + 1) * LANES]
        hi = qc.astype(BF16)
        lo = (qc - hi.astype(F32)).astype(BF16)
        keys = k_ref[cidx % 2]
        s_ref[cidx] = _dot_nt(keys, hi) + _dot_nt(keys, lo)


def _peer_scores(h2, gain, w_q, keys):
    T, D = h2.shape
    N = w_q.shape[1]
    tm, tn = ROW_TILE, COL_TILE
    per = tn // LANES
    return pl.pallas_call(
        _peer_score_kernel,
        grid=(T // tm, N // tn),
        in_specs=[
            pl.BlockSpec((tm, D), lambda i, j: (i, 0)),
            pl.BlockSpec((1, D), lambda i, j: (0, 0)),
            pl.BlockSpec((D, tn), lambda i, j: (0, j)),
            pl.BlockSpec(keys.shape, lambda i, j: (0, 0, 0)),
        ],
        out_specs=[pl.BlockSpec((per, PEER_N_KEYS, tm), lambda i, j: (j, 0, i)),
                   pl.BlockSpec((tm, D), lambda i, j: (i, 0))],
        out_shape=[jax.ShapeDtypeStruct((N // LANES, PEER_N_KEYS, T), F32),
                   jax.ShapeDtypeStruct((T, D), F32)],
        scratch_shapes=[pltpu.VMEM((tm, D), BF16)],
        compiler_params=_params("arbitrary", "arbitrary"),
        name="peer_scores",
    )(h2, gain, w_q, keys)


PAIR_COUNTS = tuple(PEER_TOPK // (a + 1) for a in range(PEER_TOPK))
N_PAIRS = sum(PAIR_COUNTS)
PAIR_ROWS = -(-N_PAIRS // SUBLANES) * SUBLANES


def _first_max(v, rows):
    m = jnp.max(v, axis=0, keepdims=True)
    idx = jnp.min(jnp.where(v == m, rows, float(v.shape[0])), axis=0, keepdims=True)
    return m, idx


def _peer_topk_kernel(s_ref, rep_ref, idx_ref, gate_ref, s_scr, m_scr, i_scr, cs_scr, ci_scr,
                      bs_scr, be_scr):
    h = pl.program_id(1)
    tt = s_ref.shape[2]
    k = PEER_TOPK
    rows_k = lax.broadcasted_iota(I32, (PEER_N_KEYS, tt), 0).astype(F32)
    rows_p = lax.broadcasted_iota(I32, (PAIR_ROWS, tt), 0).astype(F32)

    s_scr[...] = s_ref[...]

    def stage1(a, carry):
        for half in range(2):
            v = s_scr[half]
            m, idx = _first_max(v, rows_k)
            s_scr[half] = jnp.where(rows_k == idx, NEG_INF, v)
            m_scr[half, pl.ds(a, 1), :] = m
            i_scr[half, pl.ds(a, 1), :] = idx
        return carry

    lax.fori_loop(0, k, stage1, 0)

    off = 0
    for a, nb in enumerate(PAIR_COUNTS):
        cs_scr[off:off + nb, :] = m_scr[0, a:a + 1, :] + m_scr[1, 0:nb, :]
        ci_scr[off:off + nb, :] = i_scr[0, a:a + 1, :] * float(PEER_N_KEYS) + i_scr[1, 0:nb, :]
        off += nb
    cs_scr[N_PAIRS:PAIR_ROWS, :] = jnp.full((PAIR_ROWS - N_PAIRS, tt), NEG_INF, F32)
    ci_scr[N_PAIRS:PAIR_ROWS, :] = jnp.zeros((PAIR_ROWS - N_PAIRS, tt), F32)
    ci = ci_scr[...]

    def stage2(n, carry):
        cs = cs_scr[...]
        m, pos = _first_max(cs, rows_p)
        hit = rows_p == pos
        cs_scr[...] = jnp.where(hit, NEG_INF, cs)
        bs_scr[pl.ds(h * k + n, 1), :] = m
        be_scr[pl.ds(h * k + n, 1), :] = jnp.max(jnp.where(hit, ci, -1.0), axis=0, keepdims=True)
        return carry

    lax.fori_loop(0, k, stage2, 0)

    @pl.when(h == PEER_HEADS - 1)
    def _():
        gates = []
        for hh in range(PEER_HEADS):
            blk = bs_scr[hh * k:(hh + 1) * k, :]
            p = jnp.exp(blk - jnp.max(blk, axis=0, keepdims=True))
            gates.append(p / jnp.sum(p, axis=0, keepdims=True))
        gate = jnp.concatenate(gates, axis=0).T
        gate_ref[...] = _dot_hilo(gate, rep_ref[...])
        idx_ref[...] = (be_scr[...].T * float(WORD_ROWS)).astype(I32)


def _peer_topk(scores_t, rep):
    n_chunks, n_keys, T = scores_t.shape
    tt = 256
    return pl.pallas_call(
        _peer_topk_kernel,
        grid=(T // tt, PEER_HEADS),
        in_specs=[pl.BlockSpec((2, n_keys, tt), lambda i, h: (h, 0, i)),
                  pl.BlockSpec(rep.shape, lambda i, h: (0, 0))],
        out_specs=[pl.BlockSpec((tt, PEER_SEL), lambda i, h: (i, 0)),
                   pl.BlockSpec((tt, PEER_SEL * SUBLANES), lambda i, h: (i, 0))],
        out_shape=[jax.ShapeDtypeStruct((T, PEER_SEL), I32),
                   jax.ShapeDtypeStruct((T, PEER_SEL * SUBLANES), F32)],
        scratch_shapes=[pltpu.VMEM((2, n_keys, tt), F32),
                        pltpu.VMEM((2, PEER_TOPK, tt), F32), pltpu.VMEM((2, PEER_TOPK, tt), F32),
                        pltpu.VMEM((PAIR_ROWS, tt), F32), pltpu.VMEM((PAIR_ROWS, tt), F32),
                        pltpu.VMEM((PEER_SEL, tt), F32), pltpu.VMEM((PEER_SEL, tt), F32)],
        compiler_params=_params("arbitrary", "arbitrary"),
        name="peer_topk",
    )(scores_t, rep)


def _gather_experts(idx_ref, t, tab_ref, stage_ref):
    for kk in range(PEER_SEL):
        r0 = pl.multiple_of(idx_ref[t, kk], WORD_ROWS)
        stage_ref[kk * WORD_ROWS:(kk + 1) * WORD_ROWS, :] = tab_ref[pl.ds(r0, WORD_ROWS), :]


def _staged(stage_ref):
    return pltpu.bitcast(stage_ref[...], BF16)


def _pipelined_tokens(idx_ref, tab_ref, stage_a, stage_b, n_tokens, compute):
    _gather_experts(idx_ref, 0, tab_ref, stage_a)

    def body(i, carry):
        t = 2 * i
        _gather_experts(idx_ref, t + 1, tab_ref, stage_b)
        compute(t, _staged(stage_a))
        _gather_experts(idx_ref, jnp.minimum(t + 2, n_tokens - 1), tab_ref, stage_a)
        compute(t + 1, _staged(stage_b))
        return carry

    lax.fori_loop(0, n_tokens // 2, body, 0)


def _diag_mask():
    n = PEER_SEL * SUBLANES
    row = lax.broadcasted_iota(I32, (SUBLANES, n), 0)
    col = lax.broadcasted_iota(I32, (SUBLANES, n), 1)
    return (col % SUBLANES == row).astype(F32)


def _peer_down_kernel(idx_ref, x_ref, gate_ref, tab_ref, sel_ref, coef_ref, stage_a, stage_b, r_ref):
    tb = x_ref.shape[0]
    mask = _diag_mask()

    def compute(t, w):
        y = _dot_nt(x_ref[t].astype(BF16), w)
        r_ref[pl.ds(t, 1), :] = jnp.sum(y * mask, axis=0, keepdims=True)

    _pipelined_tokens(idx_ref, tab_ref, stage_a, stage_b, tb, compute)
    act = _dot_hilo(r_ref[...], sel_ref[...])
    gelu = 0.5 * act * (1.0 + lax.erf(act * (2.0 ** -0.5)))
    coef_ref[...] = gate_ref[...] * gelu


def _peer_down(idx, x3, gate_rep, tab, sel):
    T = idx.shape[0]
    tb = 64
    n = PEER_SEL * SUBLANES
    return pl.pallas_call(
        _peer_down_kernel,
        grid=(T // tb,),
        in_specs=[
            pl.BlockSpec((tb, PEER_SEL), lambda i: (i, 0), memory_space=pltpu.SMEM),
            pl.BlockSpec((tb, SUBLANES, LANES), lambda i: (i, 0, 0)),
            pl.BlockSpec((tb, n), lambda i: (i, 0)),
            pl.BlockSpec(memory_space=pltpu.VMEM),
            pl.BlockSpec(memory_space=pltpu.VMEM),
        ],
        out_specs=pl.BlockSpec((tb, n), lambda i: (i, 0)),
        out_shape=jax.ShapeDtypeStruct((T, n), F32),
        scratch_shapes=[pltpu.VMEM((PEER_SEL * WORD_ROWS, LANES), jnp.uint32), pltpu.VMEM((PEER_SEL * WORD_ROWS, LANES), jnp.uint32),
                        pltpu.VMEM((tb, n), F32)],
        compiler_params=_params("arbitrary"),
        name="peer_down",
    )(idx, x3, gate_rep, tab, sel)


def _peer_up_kernel(idx_ref, coef_ref, h_ref, tab_ref, o_ref, stage_a, stage_b):
    tb = h_ref.shape[0]
    mask = _diag_mask()

    def compute(t, w):
        cm = (coef_ref[pl.ds(t, 1), :] * mask).astype(BF16)
        o_ref[t] = h_ref[t] + _dot(cm, w)

    _pipelined_tokens(idx_ref, tab_ref, stage_a, stage_b, tb, compute)


def _peer_up(idx, coef_rep, h3, tab):
    T = idx.shape[0]
    tb = 64
    n = PEER_SEL * SUBLANES
    return pl.pallas_call(
        _peer_up_kernel,
        grid=(T // tb,),
        in_specs=[
            pl.BlockSpec((tb, PEER_SEL), lambda i: (i, 0), memory_space=pltpu.SMEM),
            pl.BlockSpec((tb, n), lambda i: (i, 0)),
            pl.BlockSpec((tb, SUBLANES, LANES), lambda i: (i, 0, 0)),
            pl.BlockSpec(memory_space=pltpu.VMEM),
        ],
        out_specs=pl.BlockSpec((tb, SUBLANES, LANES), lambda i: (i, 0, 0)),
        out_shape=jax.ShapeDtypeStruct((T, SUBLANES, LANES), F32),
        scratch_shapes=[pltpu.VMEM((PEER_SEL * WORD_ROWS, LANES), jnp.uint32), pltpu.VMEM((PEER_SEL * WORD_ROWS, LANES), jnp.uint32)],
        compiler_params=_params("arbitrary"),
        name="peer_up",
    )(idx, coef_rep, h3, tab)


def _pack_table(tab):
    e, d = tab.shape
    t = tab.astype(BF16).reshape(e, WORD_ROWS, 2, LANES)
    t = jnp.swapaxes(t, 2, 3)
    return lax.bitcast_convert_type(t, jnp.uint32).reshape(e * WORD_ROWS, LANES)


def _peer_layer(h2, gain, w_q, sub_keys, tab_u, tab_v):
    T, D = h2.shape
    n_rep = PEER_SEL * SUBLANES
    rep_lane = jnp.arange(n_rep)
    rep = (jnp.arange(PEER_SEL)[:, None] == rep_lane[None, :] // SUBLANES).astype(BF16)
    sel = (rep_lane[:, None] // SUBLANES == rep_lane[None, :] // SUBLANES).astype(BF16)
    scores, xn = _peer_scores(h2, gain.reshape(1, D), w_q.astype(BF16), sub_keys.astype(BF16))
    idx, gate_rep = _peer_topk(scores, rep)
    x3 = xn.reshape(T, SUBLANES, LANES)
    coef_rep = _peer_down(idx, x3, gate_rep, _pack_table(tab_u), sel)
    out3 = _peer_up(idx, coef_rep, h2.reshape(T, SUBLANES, LANES), _pack_table(tab_v))
    return out3.reshape(T, D)


def _even_layer(h2, gain, w_in, pool_w, pool_scale, q_gain, k_gain, w_out, batch, seq_len):
    T, D = h2.shape
    aw = ATTN_HEADS * ATTN_HEAD_DIM
    lane = jnp.arange(aw)
    bd = ((lane[:, None] // ATTN_HEAD_DIM == lane[None, :] // ATTN_HEAD_DIM)
          .astype(F32) / ATTN_HEAD_DIM).astype(BF16)
    qk_gain = jnp.stack([jnp.tile(q_gain, ATTN_HEADS), jnp.tile(k_gain, ATTN_HEADS)])
    proj = _even_in(h2, gain, w_in.astype(BF16), pool_w.astype(BF16),
                    pool_scale.reshape(1, -1), qk_gain, bd, seq_len)
    proj3 = proj.reshape(batch, seq_len, -1)
    outs, lses = [], []
    for _, dilation in DILATED_BRANCHES:
        o, lse = _attn_branch(proj3, dilation)
        outs.append(o.reshape(T, aw))
        lses.append(lse.reshape(T, aw))
    return _even_out(proj, outs, lses, w_out.astype(BF16), h2)


def _ssm_layer(h2, gain, w_in, conv_w, conv_b, dt_bias, a_log, d_skip, gate_gain, w_out,
               batch, seq_len):
    inner = w_out.shape[0]
    n_heads = inner // SSM_HEAD_DIM
    main_w = 2 * inner + 2 * SSM_GROUPS * SSM_STATE
    zxbc = _ssm_in(h2, gain, w_in[:, :main_w].astype(BF16), conv_w, conv_b.reshape(1, -1),
                   seq_len, inner)
    w_dt = jnp.pad(w_in[:, main_w:], ((0, 0), (0, LANES - n_heads))).astype(BF16)
    b_dt = jnp.pad(dt_bias, (0, LANES - n_heads)).reshape(1, LANES)
    dt = _dt_proj(h2, gain, w_dt, b_dt)
    a_row = jnp.pad(-jnp.exp(a_log), (0, LANES - n_heads)).reshape(1, LANES)
    dskip_x = jnp.repeat(d_skip, SSM_HEAD_DIM).reshape(1, inner)
    expand = (jnp.arange(LANES)[:, None] == jnp.arange(inner)[None, :] // SSM_HEAD_DIM).astype(BF16)
    y = _ssd(zxbc, dt, a_row, dskip_x, gate_gain.reshape(1, inner), expand, batch, inner)
    return _proj_res(y, w_out.astype(BF16), h2)


def kernel(x, attn_norm, ffn_norm, even_w_in, pool_w, pool_scale, q_gain, k_gain, even_w_out, ssm_w_in, conv_w, conv_b, dt_bias, a_log, d_skip, gate_gain, ssm_w_out, peer_w_q, peer_sub_keys, peer_u, peer_v):
    B, S, D = x.shape
    T = B * S
    assert D == SUBLANES * LANES and S % ROW_TILE == 0
    assert all(S % (d * ATTN_BLOCK) == 0 for _, d in DILATED_BRANCHES)
    assert all(w // d == ATTN_BLOCK for w, d in DILATED_BRANCHES)
    h = x.reshape(T, D)
    for layer in range(attn_norm.shape[0]):
        i = layer // 2
        gain = attn_norm[layer].reshape(1, D)
        if layer % 2 == 0:
            h = _even_layer(h, gain, even_w_in[i], pool_w[i], pool_scale[i], q_gain[i], k_gain[i],
                            even_w_out[i], B, S)
        else:
            h = _ssm_layer(h, gain, ssm_w_in[i], conv_w[i], conv_b[i], dt_bias[i], a_log[i],
                           d_skip[i], gate_gain[i], ssm_w_out[i], B, S)
        h = _peer_layer(h, ffn_norm[layer], peer_w_q[layer], peer_sub_keys[layer],
                        peer_u[layer], peer_v[layer])
    return h.reshape(B, S, D)
```

```python
import functools
import math

import jax
import jax.numpy as jnp
from jax import lax
from jax.experimental import pallas as pl
from jax.experimental.pallas import tpu as pltpu

F32 = jnp.float32
BF16 = jnp.bfloat16
I32 = jnp.int32
EPS = 1e-6
NEG_INF = float("-inf")

LANES = 128
SUBLANES = 8
VMEM_LIMIT = 56 * 1024 * 1024

POOL_WINDOWS = (2, 4, 8, 16)
POOL_HALO = 16
ATTN_HEADS = 8
ATTN_HEAD_DIM = 64
ATTN_BLOCK = 128
DILATED_BRANCHES = ((128, 1), (512, 4), (2048, 16))
SSM_HEAD_DIM = 64
SSM_GROUPS = 8
SSM_STATE = 128
SSM_CONV = 4
SSM_CHUNK = 128
CONV_HALO = 8
PEER_HEADS = 8
PEER_N_KEYS = 128
PEER_TOPK = 16
PEER_SEL = PEER_HEADS * PEER_TOPK
WORD_ROWS = 4
TOKEN_UNROLL = 8
ROW_TILE = 512
COL_TILE = 512


def _params(*sem):
    return pltpu.CompilerParams(dimension_semantics=sem, vmem_limit_bytes=VMEM_LIMIT)


def _dot(a, b):
    return jnp.dot(a, b, preferred_element_type=F32)


def _dot_nt(a, b):
    return lax.dot_general(a, b, (((1,), (1,)), ((), ())), preferred_element_type=F32)


def _dot_hilo(a, b_bf16):
    hi = a.astype(BF16)
    lo = (a - hi.astype(F32)).astype(BF16)
    return _dot(hi, b_bf16) + _dot(lo, b_bf16)


def _rms_rows(x, gain):
    ms = jnp.mean(x * x, axis=-1, keepdims=True)
    return x * lax.rsqrt(ms + EPS) * gain


def _sigmoid(x):
    return 1.0 / (1.0 + jnp.exp(-x))


def _window_sum(ext, w, rows):
    e = ext
    span = 1
    while span < w:
        n = e.shape[0]
        e = e[span:n, :] + e[0:n - span, :]
        span *= 2
    start = POOL_HALO + 1 - w
    return e[start:start + rows, :]


def _even_in_kernel(seq_len, h_ref, g_ref, w_ref, pw_ref, ps_ref, qkg_ref, bd_ref, o_ref,
                    xn_ref, carry_ref):
    i = pl.program_id(0)
    j = pl.program_id(1)
    tm = h_ref.shape[0]

    @pl.when(j == 0)
    def _():
        xn_ref[...] = _rms_rows(h_ref[...], g_ref[...]).astype(BF16)

    acc = _dot(xn_ref[...], w_ref[...])

    @pl.when(j == 0)
    def _():
        pos0 = (i * tm) % seq_len

        @pl.when(pos0 == 0)
        def _():
            carry_ref[...] = jnp.zeros_like(carry_ref)

        ext = jnp.concatenate([carry_ref[...], acc], axis=0)
        pos = pos0 + lax.broadcasted_iota(I32, (tm, 1), 0)
        outs = []
        for gi, w in enumerate(POOL_WINDOWS):
            sl = slice(gi * LANES, (gi + 1) * LANES)
            ws = _window_sum(ext[:, sl], w, tm)
            cnt = jnp.minimum(pos + 1, w).astype(F32)
            mixed = ws / cnt - acc[:, sl]
            outs.append(_dot(mixed.astype(BF16), pw_ref[gi]))
        o_ref[...] = jnp.concatenate(outs, axis=1) * ps_ref[...]
        carry_ref[...] = acc[tm - POOL_HALO:tm, :]

    def qk_norm(row):
        ms = _dot_hilo(acc * acc, bd_ref[...])
        o_ref[...] = acc * lax.rsqrt(ms + EPS) * qkg_ref[row:row + 1, :]

    @pl.when(j == 1)
    def _():
        qk_norm(0)

    @pl.when(j == 2)
    def _():
        qk_norm(1)

    @pl.when(j == 3)
    def _():
        o_ref[...] = acc


def _even_in(h2, gain, w_in, pool_w, pool_scale, qk_gain, bd, seq_len):
    T, D = h2.shape
    N = w_in.shape[1]
    tm, tn = ROW_TILE, COL_TILE
    return pl.pallas_call(
        functools.partial(_even_in_kernel, seq_len),
        grid=(T // tm, N // tn),
        in_specs=[
            pl.BlockSpec((tm, D), lambda i, j: (i, 0)),
            pl.BlockSpec((1, D), lambda i, j: (0, 0)),
            pl.BlockSpec((D, tn), lambda i, j: (0, j)),
            pl.BlockSpec(pool_w.shape, lambda i, j: (0, 0, 0)),
            pl.BlockSpec((1, tn), lambda i, j: (0, 0)),
            pl.BlockSpec((2, tn), lambda i, j: (0, 0)),
            pl.BlockSpec((tn, tn), lambda i, j: (0, 0)),
        ],
        out_specs=pl.BlockSpec((tm, tn), lambda i, j: (i, j)),
        out_shape=jax.ShapeDtypeStruct((T, N), F32),
        scratch_shapes=[pltpu.VMEM((tm, D), BF16), pltpu.VMEM((POOL_HALO, tn), F32)],
        compiler_params=_params("arbitrary", "arbitrary"),
        name="even_in",
    )(h2, gain, w_in, pool_w, pool_scale, qk_gain, bd)


def _attn_kernel(dilation, q_ref, kp_ref, kc_ref, vp_ref, vc_ref, o_ref, lse_ref):
    n = pl.program_id(2)
    qb = ATTN_BLOCK
    qi = lax.broadcasted_iota(I32, (qb, qb), 0)
    kj = lax.broadcasted_iota(I32, (qb, qb), 1)
    lane = lax.broadcasted_iota(I32, (qb, LANES), 1)
    valid_c = kj <= qi
    valid_p = (kj >= qi) & (n > 0)
    dist_c = ((qi - kj) * dilation).astype(F32)
    dist_p = ((qi + qb - kj) * dilation).astype(F32)
    scale = ATTN_HEAD_DIM ** -0.5
    heads_per_tile = LANES // ATTN_HEAD_DIM
    for hp in range(ATTN_HEADS // heads_per_tile):
        sl = slice(hp * LANES, (hp + 1) * LANES)
        qp = q_ref[:, sl]
        kc = kc_ref[:, sl].astype(BF16)
        kp = kp_ref[:, sl].astype(BF16)
        vc = vc_ref[:, sl].astype(BF16)
        vp = vp_ref[:, sl].astype(BF16)
        o_tile = jnp.zeros((qb, LANES), F32)
        lse_tile = jnp.zeros((qb, LANES), F32)
        for e in range(heads_per_tile):
            head = hp * heads_per_tile + e
            slope = 2.0 ** (-8.0 * (head + 1) / ATTN_HEADS)
            in_head = (lane >= e * ATTN_HEAD_DIM) & (lane < (e + 1) * ATTN_HEAD_DIM)
            qm = jnp.where(in_head, qp, 0.0).astype(BF16)
            s_c = _dot_nt(qm, kc) * scale - slope * dist_c
            s_p = _dot_nt(qm, kp) * scale - slope * dist_p
            s_c = jnp.where(valid_c, s_c, NEG_INF)
            s_p = jnp.where(valid_p, s_p, NEG_INF)
            m = jnp.maximum(jnp.max(s_c, axis=-1, keepdims=True),
                            jnp.max(s_p, axis=-1, keepdims=True))
            p_c = jnp.exp(s_c - m)
            p_p = jnp.exp(s_p - m)
            l = jnp.sum(p_c, axis=-1, keepdims=True) + jnp.sum(p_p, axis=-1, keepdims=True)
            o = (_dot(p_c.astype(BF16), vc) + _dot(p_p.astype(BF16), vp)) / l
            lse = m + jnp.log(l)
            o_tile = jnp.where(in_head, o, o_tile)
            lse_tile = jnp.where(in_head, lse, lse_tile)
        o_ref[:, sl] = o_tile
        lse_ref[:, sl] = lse_tile


def _attn_branch(proj, dilation):
    B, S, W = proj.shape
    aw = ATTN_HEADS * ATTN_HEAD_DIM
    L = S // dilation
    nb = L // ATTN_BLOCK
    cols = W // aw
    pv = proj.reshape(B, L, dilation * W)
    blk = (None, ATTN_BLOCK, aw)

    def spec(col, prev):
        if prev:
            return pl.BlockSpec(blk, lambda b, r, n: (b, jnp.maximum(n - 1, 0), r * cols + col))
        return pl.BlockSpec(blk, lambda b, r, n: (b, n, r * cols + col))

    out_spec = pl.BlockSpec(blk, lambda b, r, n: (b, n, r))
    o, lse = pl.pallas_call(
        functools.partial(_attn_kernel, dilation),
        grid=(B, dilation, nb),
        in_specs=[spec(1, False), spec(2, True), spec(2, False), spec(3, True), spec(3, False)],
        out_specs=[out_spec, out_spec],
        out_shape=[jax.ShapeDtypeStruct((B, L, dilation * aw), F32)] * 2,
        compiler_params=_params("arbitrary", "arbitrary", "arbitrary"),
        name="attn",
    )(pv, pv, pv, pv, pv)
    return o.reshape(B, S, aw), lse.reshape(B, S, aw)


def _even_out_kernel(a_ref, o1, o2, o3, l1, l2, l3, wa_ref, wo_ref, h_ref, out_ref, mix_ref):
    j = pl.program_id(1)
    half = a_ref.shape[1]

    @pl.when(j == 0)
    def _():
        la, lb, lc = l1[...], l2[...], l3[...]
        mx = jnp.maximum(jnp.maximum(la, lb), lc)
        wa, wb, wc = jnp.exp(la - mx), jnp.exp(lb - mx), jnp.exp(lc - mx)
        den = wa + wb + wc
        o = (wa / den) * o1[...] + (wb / den) * o2[...] + (wc / den) * o3[...]
        mix_ref[:, 0:half] = a_ref[...].astype(BF16)
        mix_ref[:, half:2 * half] = o.astype(BF16)

    out_ref[...] = (h_ref[...] + _dot(mix_ref[:, 0:half], wa_ref[...])
                    + _dot(mix_ref[:, half:2 * half], wo_ref[...]))


def _even_out(proj2, os_, lses, w_out, h2):
    T, D = h2.shape
    half = w_out.shape[0] // 2
    tm, tn = ROW_TILE, COL_TILE
    row = pl.BlockSpec((tm, half), lambda i, j: (i, 0))
    return pl.pallas_call(
        _even_out_kernel,
        grid=(T // tm, D // tn),
        in_specs=[row] * 7 + [
            pl.BlockSpec((half, tn), lambda i, j: (0, j)),
            pl.BlockSpec((half, tn), lambda i, j: (1, j)),
            pl.BlockSpec((tm, tn), lambda i, j: (i, j)),
        ],
        out_specs=pl.BlockSpec((tm, tn), lambda i, j: (i, j)),
        out_shape=jax.ShapeDtypeStruct((T, D), F32),
        scratch_shapes=[pltpu.VMEM((tm, 2 * half), BF16)],
        compiler_params=_params("arbitrary", "arbitrary"),
        name="even_out",
    )(proj2, *os_, *lses, w_out, w_out, h2)


def _ssm_in_kernel(seq_len, n_plain, h_ref, g_ref, w_ref, cw_ref, cb_ref, o_ref, xn_ref, carry_ref):
    i = pl.program_id(0)
    j = pl.program_id(1)
    tm = h_ref.shape[0]

    @pl.when(j == 0)
    def _():
        xn_ref[...] = _rms_rows(h_ref[...], g_ref[...]).astype(BF16)

    acc = _dot(xn_ref[...], w_ref[...])

    @pl.when(j < n_plain)
    def _():
        o_ref[...] = acc

    @pl.when(j >= n_plain)
    def _():
        jc = j - n_plain
        pos0 = (i * tm) % seq_len

        @pl.when(pos0 == 0)
        def _():
            carry_ref[jc] = jnp.zeros(carry_ref.shape[1:], F32)

        ext = jnp.concatenate([carry_ref[jc], acc], axis=0)
        y = cb_ref[...]
        for k in range(SSM_CONV):
            off = CONV_HALO - (SSM_CONV - 1) + k
            y = y + cw_ref[k:k + 1, :] * ext[off:off + tm, :]
        o_ref[...] = y * _sigmoid(y)
        carry_ref[jc] = acc[tm - CONV_HALO:tm, :]


def _ssm_in(h2, gain, w_main, conv_w, conv_b, seq_len, inner):
    T, D = h2.shape
    N = w_main.shape[1]
    tm, tn = ROW_TILE, COL_TILE
    n_plain = inner // tn
    n_conv = N // tn - n_plain
    return pl.pallas_call(
        functools.partial(_ssm_in_kernel, seq_len, n_plain),
        grid=(T // tm, N // tn),
        in_specs=[
            pl.BlockSpec((tm, D), lambda i, j: (i, 0)),
            pl.BlockSpec((1, D), lambda i, j: (0, 0)),
            pl.BlockSpec((D, tn), lambda i, j: (0, j)),
            pl.BlockSpec((SSM_CONV, tn), lambda i, j: (0, jnp.maximum(j - n_plain, 0))),
            pl.BlockSpec((1, tn), lambda i, j: (0, jnp.maximum(j - n_plain, 0))),
        ],
        out_specs=pl.BlockSpec((tm, tn), lambda i, j: (i, j)),
        out_shape=jax.ShapeDtypeStruct((T, N), F32),
        scratch_shapes=[pltpu.VMEM((tm, D), BF16), pltpu.VMEM((n_conv, CONV_HALO, tn), F32)],
        compiler_params=_params("arbitrary", "arbitrary"),
        name="ssm_in",
    )(h2, gain, w_main, conv_w, conv_b)


def _dt_kernel(h_ref, g_ref, w_ref, b_ref, o_ref):
    xn = _rms_rows(h_ref[...], g_ref[...]).astype(BF16)
    raw = _dot(xn, w_ref[...]) + b_ref[...]
    o_ref[...] = jnp.maximum(raw, 0.0) + jnp.log(1.0 + jnp.exp(-jnp.abs(raw)))


def _dt_proj(h2, gain, w_dt, dt_bias):
    T, D = h2.shape
    tm = ROW_TILE
    return pl.pallas_call(
        _dt_kernel,
        grid=(T // tm,),
        in_specs=[
            pl.BlockSpec((tm, D), lambda i: (i, 0)),
            pl.BlockSpec((1, D), lambda i: (0, 0)),
            pl.BlockSpec((D, LANES), lambda i: (0, 0)),
            pl.BlockSpec((1, LANES), lambda i: (0, 0)),
        ],
        out_specs=pl.BlockSpec((tm, LANES), lambda i: (i, 0)),
        out_shape=jax.ShapeDtypeStruct((T, LANES), F32),
        compiler_params=_params("arbitrary"),
        name="dt",
    )(h2, gain, w_dt, dt_bias)


def _ssd_kernel(z_ref, x_ref, b_ref, c_ref, dt_ref, a_ref, dsk_ref, gg_ref, ex_ref, o_ref, st_ref):
    c = pl.program_id(1)
    q = SSM_CHUNK
    gw = x_ref.shape[1] // SSM_GROUPS
    hpg = gw // SSM_HEAD_DIM

    @pl.when(c == 0)
    def _():
        st_ref[...] = jnp.zeros_like(st_ref)

    li = lax.broadcasted_iota(I32, (q, q), 0)
    si = lax.broadcasted_iota(I32, (q, q), 1)
    causal = si <= li
    tril = jnp.where(causal, 1.0, 0.0).astype(BF16)

    dt = dt_ref[...]
    da = dt * a_ref[...]
    acum = _dot_hilo_rhs(tril, da)
    acum_t = acum.T
    ex = ex_ref[...]
    dt_x = _dot_hilo(dt, ex)
    acum_x = _dot_hilo(acum, ex)
    last_x = acum_x[q - 1:q, :]
    x = x_ref[...]
    xdt = x * dt_x
    grow = jnp.exp(acum_x)
    xw = (xdt * jnp.exp(last_x - acum_x)).astype(BF16)
    sdec = jnp.exp(last_x)
    xdt_b = xdt.astype(BF16)

    for g in range(SSM_GROUPS):
        gs = slice(g * gw, (g + 1) * gw)
        bg = b_ref[:, g * SSM_STATE:(g + 1) * SSM_STATE]
        cg = c_ref[:, g * SSM_STATE:(g + 1) * SSM_STATE].astype(BF16)
        cb = _dot_nt(cg, bg.astype(BF16))
        bg_t = bg.T.astype(BF16)
        yd = []
        for r in range(hpg):
            hh = g * hpg + r
            seg = acum[:, hh:hh + 1] - acum_t[hh:hh + 1, :]
            decay = jnp.exp(jnp.where(causal, seg, NEG_INF))
            m = (cb * decay).astype(BF16)
            yd.append(_dot(m, xdt_b[:, g * gw + r * SSM_HEAD_DIM:g * gw + (r + 1) * SSM_HEAD_DIM]))
        y = jnp.concatenate(yd, axis=1)
        st = st_ref[g]
        y = y + _dot(cg, st.astype(BF16)) * grow[:, gs]
        st_ref[g] = st * sdec[:, gs] + _dot(bg_t, xw[:, gs])
        y = y + dsk_ref[:, gs] * x[:, gs]
        zg = z_ref[:, gs]
        y = y * (zg * _sigmoid(zg))
        o_ref[:, gs] = _rms_rows(y, gg_ref[:, gs])


def _dot_hilo_rhs(a_bf16, b):
    hi = b.astype(BF16)
    lo = (b - hi.astype(F32)).astype(BF16)
    return _dot(a_bf16, hi) + _dot(a_bf16, lo)


def _ssd(zxbc, dt, a_row, dskip_x, gate_gain, expand, batch, inner):
    T = zxbc.shape[0]
    q = SSM_CHUNK
    nc = T // batch // q
    bw = SSM_GROUPS * SSM_STATE
    row = lambda col: (lambda b, c: (b * nc + c, col))
    full = lambda b, c: (0, 0)
    return pl.pallas_call(
        _ssd_kernel,
        grid=(batch, nc),
        in_specs=[
            pl.BlockSpec((q, inner), row(0)),
            pl.BlockSpec((q, inner), row(1)),
            pl.BlockSpec((q, bw), row(2 * inner // bw)),
            pl.BlockSpec((q, bw), row(2 * inner // bw + 1)),
            pl.BlockSpec((q, LANES), row(0)),
            pl.BlockSpec((1, LANES), full),
            pl.BlockSpec((1, inner), full),
            pl.BlockSpec((1, inner), full),
            pl.BlockSpec((LANES, inner), full),
        ],
        out_specs=pl.BlockSpec((q, inner), row(0)),
        out_shape=jax.ShapeDtypeStruct((T, inner), F32),
        scratch_shapes=[pltpu.VMEM((SSM_GROUPS, SSM_STATE, inner // SSM_GROUPS), F32)],
        compiler_params=_params("arbitrary", "arbitrary"),
        name="ssd",
    )(zxbc, zxbc, zxbc, zxbc, dt, a_row, dskip_x, gate_gain, expand)


def _proj_res_kernel(x_ref, w_ref, h_ref, o_ref):
    o_ref[...] = h_ref[...] + _dot(x_ref[...].astype(BF16), w_ref[...])


def _proj_res(x2, w, h2):
    T, K = x2.shape
    D = w.shape[1]
    tm, tn = ROW_TILE, COL_TILE
    return pl.pallas_call(
        _proj_res_kernel,
        grid=(T // tm, D // tn),
        in_specs=[
            pl.BlockSpec((tm, K), lambda i, j: (i, 0)),
            pl.BlockSpec((K, tn), lambda i, j: (0, j)),
            pl.BlockSpec((tm, tn), lambda i, j: (i, j)),
        ],
        out_specs=pl.BlockSpec((tm, tn), lambda i, j: (i, j)),
        out_shape=jax.ShapeDtypeStruct((T, D), F32),
        compiler_params=_params("arbitrary", "arbitrary"),
        name="proj_res",
    )(x2, w, h2)


def _peer_score_kernel(h_ref, g_ref, w_ref, k_ref, s_ref, xn_out_ref, xn_ref):
    j = pl.program_id(1)

    @pl.when(j == 0)
    def _():
        xn = _rms_rows(h_ref[...], g_ref[...])
        xn_out_ref[...] = xn
        xn_ref[...] = xn.astype(BF16)

    qv = _dot(xn_ref[...], w_ref[...])
    for cidx in range(qv.shape[1] // LANES):
        qc = qv[:, cidx * LANES:(cidx + 1) * LANES]
        hi = qc.astype(BF16)
        lo = (qc - hi.astype(F32)).astype(BF16)
        keys = k_ref[cidx % 2]
        s_ref[cidx] = _dot_nt(keys, hi) + _dot_nt(keys, lo)


def _peer_scores(h2, gain, w_q, keys):
    T, D = h2.shape
    N = w_q.shape[1]
    tm, tn = ROW_TILE, COL_TILE
    per = tn // LANES
    return pl.pallas_call(
        _peer_score_kernel,
        grid=(T // tm, N // tn),
        in_specs=[
            pl.BlockSpec((tm, D), lambda i, j: (i, 0)),
            pl.BlockSpec((1, D), lambda i, j: (0, 0)),
            pl.BlockSpec((D, tn), lambda i, j: (0, j)),
            pl.BlockSpec(keys.shape, lambda i, j: (0, 0, 0)),
        ],
        out_specs=[pl.BlockSpec((per, PEER_N_KEYS, tm), lambda i, j: (j, 0, i)),
                   pl.BlockSpec((tm, D), lambda i, j: (i, 0))],
        out_shape=[jax.ShapeDtypeStruct((N // LANES, PEER_N_KEYS, T), F32),
                   jax.ShapeDtypeStruct((T, D), F32)],
        scratch_shapes=[pltpu.VMEM((tm, D), BF16)],
        compiler_params=_params("arbitrary", "arbitrary"),
        name="peer_scores",
    )(h2, gain, w_q, keys)


PAIR_COUNTS = tuple(PEER_TOPK // (a + 1) for a in range(PEER_TOPK))
N_PAIRS = sum(PAIR_COUNTS)
PAIR_ROWS = -(-N_PAIRS // SUBLANES) * SUBLANES


def _first_max(v, rows):
    m = jnp.max(v, axis=0, keepdims=True)
    idx = jnp.min(jnp.where(v == m, rows, float(v.shape[0])), axis=0, keepdims=True)
    return m, idx


def _peer_topk_kernel(s_ref, rep_ref, idx_ref, gate_ref, s_scr, m_scr, i_scr, cs_scr, ci_scr,
                      bs_scr, be_scr):
    h = pl.program_id(1)
    tt = s_ref.shape[2]
    k = PEER_TOPK
    rows_k = lax.broadcasted_iota(I32, (PEER_N_KEYS, tt), 0).astype(F32)
    rows_p = lax.broadcasted_iota(I32, (PAIR_ROWS, tt), 0).astype(F32)

    s_scr[...] = s_ref[...]

    def stage1(a, carry):
        for half in range(2):
            v = s_scr[half]
            m, idx = _first_max(v, rows_k)
            s_scr[half] = jnp.where(rows_k == idx, NEG_INF, v)
            m_scr[half, pl.ds(a, 1), :] = m
            i_scr[half, pl.ds(a, 1), :] = idx
        return carry

    lax.fori_loop(0, k, stage1, 0)

    off = 0
    for a, nb in enumerate(PAIR_COUNTS):
        cs_scr[off:off + nb, :] = m_scr[0, a:a + 1, :] + m_scr[1, 0:nb, :]
        ci_scr[off:off + nb, :] = i_scr[0, a:a + 1, :] * float(PEER_N_KEYS) + i_scr[1, 0:nb, :]
        off += nb
    cs_scr[N_PAIRS:PAIR_ROWS, :] = jnp.full((PAIR_ROWS - N_PAIRS, tt), NEG_INF, F32)
    ci_scr[N_PAIRS:PAIR_ROWS, :] = jnp.zeros((PAIR_ROWS - N_PAIRS, tt), F32)
    ci = ci_scr[...]

    def stage2(n, carry):
        cs = cs_scr[...]
        m, pos = _first_max(cs, rows_p)
        hit = rows_p == pos
        cs_scr[...] = jnp.where(hit, NEG_INF, cs)
        bs_scr[pl.ds(h * k + n, 1), :] = m
        be_scr[pl.ds(h * k + n, 1), :] = jnp.max(jnp.where(hit, ci, -1.0), axis=0, keepdims=True)
        return carry

    lax.fori_loop(0, k, stage2, 0)

    @pl.when(h == PEER_HEADS - 1)
    def _():
        gates = []
        for hh in range(PEER_HEADS):
            blk = bs_scr[hh * k:(hh + 1) * k, :]
            p = jnp.exp(blk - jnp.max(blk, axis=0, keepdims=True))
            gates.append(p / jnp.sum(p, axis=0, keepdims=True))
        gate = jnp.concatenate(gates, axis=0).T
        gate_ref[...] = _dot_hilo(gate, rep_ref[...])
        idx_ref[...] = (be_scr[...].T * float(WORD_ROWS)).astype(I32)


def _peer_topk(scores_t, rep):
    n_chunks, n_keys, T = scores_t.shape
    tt = 256
    return pl.pallas_call(
        _peer_topk_kernel,
        grid=(T // tt, PEER_HEADS),
        in_specs=[pl.BlockSpec((2, n_keys, tt), lambda i, h: (h, 0, i)),
                  pl.BlockSpec(rep.shape, lambda i, h: (0, 0))],
        out_specs=[pl.BlockSpec((tt, PEER_SEL), lambda i, h: (i, 0)),
                   pl.BlockSpec((tt, PEER_SEL * SUBLANES), lambda i, h: (i, 0))],
        out_shape=[jax.ShapeDtypeStruct((T, PEER_SEL), I32),
                   jax.ShapeDtypeStruct((T, PEER_SEL * SUBLANES), F32)],
        scratch_shapes=[pltpu.VMEM((2, n_keys, tt), F32),
                        pltpu.VMEM((2, PEER_TOPK, tt), F32), pltpu.VMEM((2, PEER_TOPK, tt), F32),
                        pltpu.VMEM((PAIR_ROWS, tt), F32), pltpu.VMEM((PAIR_ROWS, tt), F32),
                        pltpu.VMEM((PEER_SEL, tt), F32), pltpu.VMEM((PEER_SEL, tt), F32)],
        compiler_params=_params("arbitrary", "arbitrary"),
        name="peer_topk",
    )(scores_t, rep)


def _gather_experts(idx_ref, t, tab_ref, stage_ref):
    for kk in range(PEER_SEL):
        r0 = pl.multiple_of(idx_ref[t, kk], WORD_ROWS)
        stage_ref[kk * WORD_ROWS:(kk + 1) * WORD_ROWS, :] = tab_ref[pl.ds(r0, WORD_ROWS), :]


def _staged(stage_ref):
    return pltpu.bitcast(stage_ref[...], BF16)


def _pipelined_tokens(idx_ref, tab_ref, stage_a, stage_b, n_tokens, compute):
    stages = (stage_a, stage_b)
    _gather_experts(idx_ref, 0, tab_ref, stage_a)

    def body(i, carry):
        t0 = i * TOKEN_UNROLL
        for u in range(TOKEN_UNROLL):
            nxt = jnp.minimum(t0 + u + 1, n_tokens - 1)
            _gather_experts(idx_ref, nxt, tab_ref, stages[(u + 1) % 2])
            compute(t0 + u, _staged(stages[u % 2]))
        return carry

    lax.fori_loop(0, n_tokens // TOKEN_UNROLL, body, 0)


def _diag_mask():
    n = PEER_SEL * SUBLANES
    row = lax.broadcasted_iota(I32, (SUBLANES, n), 0)
    col = lax.broadcasted_iota(I32, (SUBLANES, n), 1)
    return (col % SUBLANES == row).astype(F32)


def _peer_down_kernel(idx_ref, x_ref, gate_ref, tab_ref, sel_ref, coef_ref, stage_a, stage_b, r_ref):
    tb = x_ref.shape[0]
    mask = _diag_mask()

    def compute(t, w):
        x_tile = x_ref[pl.ds(t, 1), :].reshape(SUBLANES, LANES)
        y = _dot_nt(x_tile.astype(BF16), w)
        r_ref[pl.ds(t, 1), :] = jnp.sum(y * mask, axis=0, keepdims=True)

    _pipelined_tokens(idx_ref, tab_ref, stage_a, stage_b, tb, compute)
    act = _dot_hilo(r_ref[...], sel_ref[...])
    gelu = 0.5 * act * (1.0 + lax.erf(act * (2.0 ** -0.5)))
    coef_ref[...] = gate_ref[...] * gelu


def _peer_down(idx, xn, gate_rep, tab, sel):
    T = idx.shape[0]
    tb = 64
    n = PEER_SEL * SUBLANES
    return pl.pallas_call(
        _peer_down_kernel,
        grid=(T // tb,),
        in_specs=[
            pl.BlockSpec((tb, PEER_SEL), lambda i: (i, 0), memory_space=pltpu.SMEM),
            pl.BlockSpec((tb, n), lambda i: (i, 0)),
            pl.BlockSpec((tb, n), lambda i: (i, 0)),
            pl.BlockSpec(memory_space=pltpu.VMEM),
            pl.BlockSpec(memory_space=pltpu.VMEM),
        ],
        out_specs=pl.BlockSpec((tb, n), lambda i: (i, 0)),
        out_shape=jax.ShapeDtypeStruct((T, n), F32),
        scratch_shapes=[pltpu.VMEM((PEER_SEL * WORD_ROWS, LANES), jnp.uint32), pltpu.VMEM((PEER_SEL * WORD_ROWS, LANES), jnp.uint32),
                        pltpu.VMEM((tb, n), F32)],
        compiler_params=_params("arbitrary"),
        name="peer_down",
    )(idx, xn, gate_rep, tab, sel)


def _peer_up_kernel(idx_ref, coef_ref, h_ref, tab_ref, o_ref, stage_a, stage_b):
    tb = h_ref.shape[0]
    mask = _diag_mask()

    def compute(t, w):
        cm = (coef_ref[pl.ds(t, 1), :] * mask).astype(BF16)
        o_tile = h_ref[pl.ds(t, 1), :].reshape(SUBLANES, LANES) + _dot(cm, w)
        o_ref[pl.ds(t, 1), :] = o_tile.reshape(1, SUBLANES * LANES)

    _pipelined_tokens(idx_ref, tab_ref, stage_a, stage_b, tb, compute)


def _peer_up(idx, coef_rep, h2, tab):
    T = idx.shape[0]
    tb = 64
    n = PEER_SEL * SUBLANES
    return pl.pallas_call(
        _peer_up_kernel,
        grid=(T // tb,),
        in_specs=[
            pl.BlockSpec((tb, PEER_SEL), lambda i: (i, 0), memory_space=pltpu.SMEM),
            pl.BlockSpec((tb, n), lambda i: (i, 0)),
            pl.BlockSpec((tb, n), lambda i: (i, 0)),
            pl.BlockSpec(memory_space=pltpu.VMEM),
        ],
        out_specs=pl.BlockSpec((tb, n), lambda i: (i, 0)),
        out_shape=jax.ShapeDtypeStruct((T, n), F32),
        scratch_shapes=[pltpu.VMEM((PEER_SEL * WORD_ROWS, LANES), jnp.uint32), pltpu.VMEM((PEER_SEL * WORD_ROWS, LANES), jnp.uint32)],
        compiler_params=_params("arbitrary"),
        name="peer_up",
    )(idx, coef_rep, h2, tab)


def _pack_table(tab):
    e, d = tab.shape
    t = tab.astype(BF16).reshape(e, WORD_ROWS, 2, LANES)
    t = jnp.swapaxes(t, 2, 3)
    return lax.bitcast_convert_type(t, jnp.uint32).reshape(e * WORD_ROWS, LANES)


def _peer_layer(h2, gain, w_q, sub_keys, tab_u, tab_v):
    T, D = h2.shape
    n_rep = PEER_SEL * SUBLANES
    rep_lane = jnp.arange(n_rep)
    rep = (jnp.arange(PEER_SEL)[:, None] == rep_lane[None, :] // SUBLANES).astype(BF16)
    sel = (rep_lane[:, None] // SUBLANES == rep_lane[None, :] // SUBLANES).astype(BF16)
    scores, xn = _peer_scores(h2, gain.reshape(1, D), w_q.astype(BF16), sub_keys.astype(BF16))
    idx, gate_rep = _peer_topk(scores, rep)
    coef_rep = _peer_down(idx, xn, gate_rep, _pack_table(tab_u), sel)
    return _peer_up(idx, coef_rep, h2, _pack_table(tab_v))


def _even_layer(h2, gain, w_in, pool_w, pool_scale, q_gain, k_gain, w_out, batch, seq_len):
    T, D = h2.shape
    aw = ATTN_HEADS * ATTN_HEAD_DIM
    lane = jnp.arange(aw)
    bd = ((lane[:, None] // ATTN_HEAD_DIM == lane[None, :] // ATTN_HEAD_DIM)
          .astype(F32) / ATTN_HEAD_DIM).astype(BF16)
    qk_gain = jnp.stack([jnp.tile(q_gain, ATTN_HEADS), jnp.tile(k_gain, ATTN_HEADS)])
    proj = _even_in(h2, gain, w_in.astype(BF16), pool_w.astype(BF16),
                    pool_scale.reshape(1, -1), qk_gain, bd, seq_len)
    proj3 = proj.reshape(batch, seq_len, -1)
    outs, lses = [], []
    for _, dilation in DILATED_BRANCHES:
        o, lse = _attn_branch(proj3, dilation)
        outs.append(o.reshape(T, aw))
        lses.append(lse.reshape(T, aw))
    return _even_out(proj, outs, lses, w_out.astype(BF16), h2)


def _ssm_layer(h2, gain, w_in, conv_w, conv_b, dt_bias, a_log, d_skip, gate_gain, w_out,
               batch, seq_len):
    inner = w_out.shape[0]
    n_heads = inner // SSM_HEAD_DIM
    main_w = 2 * inner + 2 * SSM_GROUPS * SSM_STATE
    zxbc = _ssm_in(h2, gain, w_in[:, :main_w].astype(BF16), conv_w, conv_b.reshape(1, -1),
                   seq_len, inner)
    w_dt = jnp.pad(w_in[:, main_w:], ((0, 0), (0, LANES - n_heads))).astype(BF16)
    b_dt = jnp.pad(dt_bias, (0, LANES - n_heads)).reshape(1, LANES)
    dt = _dt_proj(h2, gain, w_dt, b_dt)
    a_row = jnp.pad(-jnp.exp(a_log), (0, LANES - n_heads)).reshape(1, LANES)
    dskip_x = jnp.repeat(d_skip, SSM_HEAD_DIM).reshape(1, inner)
    expand = (jnp.arange(LANES)[:, None] == jnp.arange(inner)[None, :] // SSM_HEAD_DIM).astype(BF16)
    y = _ssd(zxbc, dt, a_row, dskip_x, gate_gain.reshape(1, inner), expand, batch, inner)
    return _proj_res(y, w_out.astype(BF16), h2)


def kernel(x, attn_norm, ffn_norm, even_w_in, pool_w, pool_scale, q_gain, k_gain, even_w_out, ssm_w_in, conv_w, conv_b, dt_bias, a_log, d_skip, gate_gain, ssm_w_out, peer_w_q, peer_sub_keys, peer_u, peer_v):
    B, S, D = x.shape
    T = B * S
    assert D == SUBLANES * LANES and S % ROW_TILE == 0
    assert all(S % (d * ATTN_BLOCK) == 0 for _, d in DILATED_BRANCHES)
    assert all(w // d == ATTN_BLOCK for w, d in DILATED_BRANCHES)
    h = x.reshape(T, D)
    for layer in range(attn_norm.shape[0]):
        i = layer // 2
        gain = attn_norm[layer].reshape(1, D)
        if layer % 2 == 0:
            h = _even_layer(h, gain, even_w_in[i], pool_w[i], pool_scale[i], q_gain[i], k_gain[i],
                            even_w_out[i], B, S)
        else:
            h = _ssm_layer(h, gain, ssm_w_in[i], conv_w[i], conv_b[i], dt_bias[i], a_log[i],
                           d_skip[i], gate_gain[i], ssm_w_out[i], B, S)
        h = _peer_layer(h, ffn_norm[layer], peer_w_q[layer], peer_sub_keys[layer],
                        peer_u[layer], peer_v[layer])
    return h.reshape(B, S, D)
```

```python
import functools
import math

import jax
import jax.numpy as jnp
from jax import lax
from jax.experimental import pallas as pl
from jax.experimental.pallas import tpu as pltpu

F32 = jnp.float32
BF16 = jnp.bfloat16
I32 = jnp.int32
EPS = 1e-6
NEG_INF = float("-inf")

LANES = 128
SUBLANES = 8
VMEM_LIMIT = 56 * 1024 * 1024

POOL_WINDOWS = (2, 4, 8, 16)
POOL_HALO = 16
ATTN_HEADS = 8
ATTN_HEAD_DIM = 64
ATTN_BLOCK = 128
DILATED_BRANCHES = ((128, 1), (512, 4), (2048, 16))
SSM_HEAD_DIM = 64
SSM_GROUPS = 8
SSM_STATE = 128
SSM_CONV = 4
SSM_CHUNK = 128
CONV_HALO = 8
PEER_HEADS = 8
PEER_N_KEYS = 128
PEER_TOPK = 16
PEER_SEL = PEER_HEADS * PEER_TOPK
WORD_ROWS = 4
IDX_HALF = 16
ROW_TILE = 512
COL_TILE = 512


def _params(*sem):
    return pltpu.CompilerParams(dimension_semantics=sem, vmem_limit_bytes=VMEM_LIMIT)


def _dot(a, b):
    return jnp.dot(a, b, preferred_element_type=F32)


def _dot_nt(a, b):
    return lax.dot_general(a, b, (((1,), (1,)), ((), ())), preferred_element_type=F32)


def _dot_hilo(a, b_bf16):
    hi = a.astype(BF16)
    lo = (a - hi.astype(F32)).astype(BF16)
    return _dot(hi, b_bf16) + _dot(lo, b_bf16)


def _rms_rows(x, gain):
    ms = jnp.mean(x * x, axis=-1, keepdims=True)
    return x * lax.rsqrt(ms + EPS) * gain


def _sigmoid(x):
    return 1.0 / (1.0 + jnp.exp(-x))


def _window_sum(ext, w, rows):
    e = ext
    span = 1
    while span < w:
        n = e.shape[0]
        e = e[span:n, :] + e[0:n - span, :]
        span *= 2
    start = POOL_HALO + 1 - w
    return e[start:start + rows, :]


def _even_in_kernel(seq_len, h_ref, g_ref, w_ref, pw_ref, ps_ref, qkg_ref, bd_ref, o_ref,
                    xn_ref, carry_ref):
    i = pl.program_id(0)
    j = pl.program_id(1)
    tm = h_ref.shape[0]

    @pl.when(j == 0)
    def _():
        xn_ref[...] = _rms_rows(h_ref[...], g_ref[...]).astype(BF16)

    acc = _dot(xn_ref[...], w_ref[...])

    @pl.when(j == 0)
    def _():
        pos0 = (i * tm) % seq_len

        @pl.when(pos0 == 0)
        def _():
            carry_ref[...] = jnp.zeros_like(carry_ref)

        ext = jnp.concatenate([carry_ref[...], acc], axis=0)
        pos = pos0 + lax.broadcasted_iota(I32, (tm, 1), 0)
        outs = []
        for gi, w in enumerate(POOL_WINDOWS):
            sl = slice(gi * LANES, (gi + 1) * LANES)
            ws = _window_sum(ext[:, sl], w, tm)
            cnt = jnp.minimum(pos + 1, w).astype(F32)
            mixed = ws / cnt - acc[:, sl]
            outs.append(_dot(mixed.astype(BF16), pw_ref[gi]))
        o_ref[...] = jnp.concatenate(outs, axis=1) * ps_ref[...]
        carry_ref[...] = acc[tm - POOL_HALO:tm, :]

    def qk_norm(row):
        ms = _dot_hilo(acc * acc, bd_ref[...])
        o_ref[...] = acc * lax.rsqrt(ms + EPS) * qkg_ref[row:row + 1, :]

    @pl.when(j == 1)
    def _():
        qk_norm(0)

    @pl.when(j == 2)
    def _():
        qk_norm(1)

    @pl.when(j == 3)
    def _():
        o_ref[...] = acc


def _even_in(h2, gain, w_in, pool_w, pool_scale, qk_gain, bd, seq_len):
    T, D = h2.shape
    N = w_in.shape[1]
    tm, tn = ROW_TILE, COL_TILE
    return pl.pallas_call(
        functools.partial(_even_in_kernel, seq_len),
        grid=(T // tm, N // tn),
        in_specs=[
            pl.BlockSpec((tm, D), lambda i, j: (i, 0)),
            pl.BlockSpec((1, D), lambda i, j: (0, 0)),
            pl.BlockSpec((D, tn), lambda i, j: (0, j)),
            pl.BlockSpec(pool_w.shape, lambda i, j: (0, 0, 0)),
            pl.BlockSpec((1, tn), lambda i, j: (0, 0)),
            pl.BlockSpec((2, tn), lambda i, j: (0, 0)),
            pl.BlockSpec((tn, tn), lambda i, j: (0, 0)),
        ],
        out_specs=pl.BlockSpec((tm, tn), lambda i, j: (i, j)),
        out_shape=jax.ShapeDtypeStruct((T, N), F32),
        scratch_shapes=[pltpu.VMEM((tm, D), BF16), pltpu.VMEM((POOL_HALO, tn), F32)],
        compiler_params=_params("arbitrary", "arbitrary"),
        name="even_in",
    )(h2, gain, w_in, pool_w, pool_scale, qk_gain, bd)


def _attn_kernel(dilation, q_ref, kp_ref, kc_ref, vp_ref, vc_ref, o_ref, lse_ref):
    n = pl.program_id(2)
    qb = ATTN_BLOCK
    qi = lax.broadcasted_iota(I32, (qb, qb), 0)
    kj = lax.broadcasted_iota(I32, (qb, qb), 1)
    lane = lax.broadcasted_iota(I32, (qb, LANES), 1)
    valid_c = kj <= qi
    valid_p = (kj >= qi) & (n > 0)
    dist_c = ((qi - kj) * dilation).astype(F32)
    dist_p = ((qi + qb - kj) * dilation).astype(F32)
    scale = ATTN_HEAD_DIM ** -0.5
    heads_per_tile = LANES // ATTN_HEAD_DIM
    for hp in range(ATTN_HEADS // heads_per_tile):
        sl = slice(hp * LANES, (hp + 1) * LANES)
        qp = q_ref[:, sl]
        kc = kc_ref[:, sl].astype(BF16)
        kp = kp_ref[:, sl].astype(BF16)
        vc = vc_ref[:, sl].astype(BF16)
        vp = vp_ref[:, sl].astype(BF16)
        o_tile = jnp.zeros((qb, LANES), F32)
        lse_tile = jnp.zeros((qb, LANES), F32)
        for e in range(heads_per_tile):
            head = hp * heads_per_tile + e
            slope = 2.0 ** (-8.0 * (head + 1) / ATTN_HEADS)
            in_head = (lane >= e * ATTN_HEAD_DIM) & (lane < (e + 1) * ATTN_HEAD_DIM)
            qm = jnp.where(in_head, qp, 0.0).astype(BF16)
            s_c = _dot_nt(qm, kc) * scale - slope * dist_c
            s_p = _dot_nt(qm, kp) * scale - slope * dist_p
            s_c = jnp.where(valid_c, s_c, NEG_INF)
            s_p = jnp.where(valid_p, s_p, NEG_INF)
            m = jnp.maximum(jnp.max(s_c, axis=-1, keepdims=True),
                            jnp.max(s_p, axis=-1, keepdims=True))
            p_c = jnp.exp(s_c - m)
            p_p = jnp.exp(s_p - m)
            l = jnp.sum(p_c, axis=-1, keepdims=True) + jnp.sum(p_p, axis=-1, keepdims=True)
            o = (_dot(p_c.astype(BF16), vc) + _dot(p_p.astype(BF16), vp)) / l
            lse = m + jnp.log(l)
            o_tile = jnp.where(in_head, o, o_tile)
            lse_tile = jnp.where(in_head, lse, lse_tile)
        o_ref[:, sl] = o_tile
        lse_ref[:, sl] = lse_tile


def _attn_branch(proj, dilation):
    B, S, W = proj.shape
    aw = ATTN_HEADS * ATTN_HEAD_DIM
    L = S // dilation
    nb = L // ATTN_BLOCK
    cols = W // aw
    pv = proj.reshape(B, L, dilation * W)
    blk = (None, ATTN_BLOCK, aw)

    def spec(col, prev):
        if prev:
            return pl.BlockSpec(blk, lambda b, r, n: (b, jnp.maximum(n - 1, 0), r * cols + col))
        return pl.BlockSpec(blk, lambda b, r, n: (b, n, r * cols + col))

    out_spec = pl.BlockSpec(blk, lambda b, r, n: (b, n, r))
    o, lse = pl.pallas_call(
        functools.partial(_attn_kernel, dilation),
        grid=(B, dilation, nb),
        in_specs=[spec(1, False), spec(2, True), spec(2, False), spec(3, True), spec(3, False)],
        out_specs=[out_spec, out_spec],
        out_shape=[jax.ShapeDtypeStruct((B, L, dilation * aw), F32)] * 2,
        compiler_params=_params("arbitrary", "arbitrary", "arbitrary"),
        name="attn",
    )(pv, pv, pv, pv, pv)
    return o.reshape(B, S, aw), lse.reshape(B, S, aw)


def _even_out_kernel(a_ref, o1, o2, o3, l1, l2, l3, wa_ref, wo_ref, h_ref, out_ref, mix_ref):
    j = pl.program_id(1)
    half = a_ref.shape[1]

    @pl.when(j == 0)
    def _():
        la, lb, lc = l1[...], l2[...], l3[...]
        mx = jnp.maximum(jnp.maximum(la, lb), lc)
        wa, wb, wc = jnp.exp(la - mx), jnp.exp(lb - mx), jnp.exp(lc - mx)
        den = wa + wb + wc
        o = (wa / den) * o1[...] + (wb / den) * o2[...] + (wc / den) * o3[...]
        mix_ref[:, 0:half] = a_ref[...].astype(BF16)
        mix_ref[:, half:2 * half] = o.astype(BF16)

    out_ref[...] = (h_ref[...] + _dot(mix_ref[:, 0:half], wa_ref[...])
                    + _dot(mix_ref[:, half:2 * half], wo_ref[...]))


def _even_out(proj2, os_, lses, w_out, h2):
    T, D = h2.shape
    half = w_out.shape[0] // 2
    tm, tn = ROW_TILE, COL_TILE
    row = pl.BlockSpec((tm, half), lambda i, j: (i, 0))
    return pl.pallas_call(
        _even_out_kernel,
        grid=(T // tm, D // tn),
        in_specs=[row] * 7 + [
            pl.BlockSpec((half, tn), lambda i, j: (0, j)),
            pl.BlockSpec((half, tn), lambda i, j: (1, j)),
            pl.BlockSpec((tm, tn), lambda i, j: (i, j)),
        ],
        out_specs=pl.BlockSpec((tm, tn), lambda i, j: (i, j)),
        out_shape=jax.ShapeDtypeStruct((T, D), F32),
        scratch_shapes=[pltpu.VMEM((tm, 2 * half), BF16)],
        compiler_params=_params("arbitrary", "arbitrary"),
        name="even_out",
    )(proj2, *os_, *lses, w_out, w_out, h2)


def _ssm_in_kernel(seq_len, n_plain, h_ref, g_ref, w_ref, cw_ref, cb_ref, o_ref, xn_ref, carry_ref):
    i = pl.program_id(0)
    j = pl.program_id(1)
    tm = h_ref.shape[0]

    @pl.when(j == 0)
    def _():
        xn_ref[...] = _rms_rows(h_ref[...], g_ref[...]).astype(BF16)

    acc = _dot(xn_ref[...], w_ref[...])

    @pl.when(j < n_plain)
    def _():
        o_ref[...] = acc

    @pl.when(j >= n_plain)
    def _():
        jc = j - n_plain
        pos0 = (i * tm) % seq_len

        @pl.when(pos0 == 0)
        def _():
            carry_ref[jc] = jnp.zeros(carry_ref.shape[1:], F32)

        ext = jnp.concatenate([carry_ref[jc], acc], axis=0)
        y = cb_ref[...]
        for k in range(SSM_CONV):
            off = CONV_HALO - (SSM_CONV - 1) + k
            y = y + cw_ref[k:k + 1, :] * ext[off:off + tm, :]
        o_ref[...] = y * _sigmoid(y)
        carry_ref[jc] = acc[tm - CONV_HALO:tm, :]


def _ssm_in(h2, gain, w_main, conv_w, conv_b, seq_len, inner):
    T, D = h2.shape
    N = w_main.shape[1]
    tm, tn = ROW_TILE, COL_TILE
    n_plain = inner // tn
    n_conv = N // tn - n_plain
    return pl.pallas_call(
        functools.partial(_ssm_in_kernel, seq_len, n_plain),
        grid=(T // tm, N // tn),
        in_specs=[
            pl.BlockSpec((tm, D), lambda i, j: (i, 0)),
            pl.BlockSpec((1, D), lambda i, j: (0, 0)),
            pl.BlockSpec((D, tn), lambda i, j: (0, j)),
            pl.BlockSpec((SSM_CONV, tn), lambda i, j: (0, jnp.maximum(j - n_plain, 0))),
            pl.BlockSpec((1, tn), lambda i, j: (0, jnp.maximum(j - n_plain, 0))),
        ],
        out_specs=pl.BlockSpec((tm, tn), lambda i, j: (i, j)),
        out_shape=jax.ShapeDtypeStruct((T, N), F32),
        scratch_shapes=[pltpu.VMEM((tm, D), BF16), pltpu.VMEM((n_conv, CONV_HALO, tn), F32)],
        compiler_params=_params("arbitrary", "arbitrary"),
        name="ssm_in",
    )(h2, gain, w_main, conv_w, conv_b)


def _dt_kernel(h_ref, g_ref, w_ref, b_ref, o_ref):
    xn = _rms_rows(h_ref[...], g_ref[...]).astype(BF16)
    raw = _dot(xn, w_ref[...]) + b_ref[...]
    o_ref[...] = jnp.maximum(raw, 0.0) + jnp.log(1.0 + jnp.exp(-jnp.abs(raw)))


def _dt_proj(h2, gain, w_dt, dt_bias):
    T, D = h2.shape
    tm = ROW_TILE
    return pl.pallas_call(
        _dt_kernel,
        grid=(T // tm,),
        in_specs=[
            pl.BlockSpec((tm, D), lambda i: (i, 0)),
            pl.BlockSpec((1, D), lambda i: (0, 0)),
            pl.BlockSpec((D, LANES), lambda i: (0, 0)),
            pl.BlockSpec((1, LANES), lambda i: (0, 0)),
        ],
        out_specs=pl.BlockSpec((tm, LANES), lambda i: (i, 0)),
        out_shape=jax.ShapeDtypeStruct((T, LANES), F32),
        compiler_params=_params("arbitrary"),
        name="dt",
    )(h2, gain, w_dt, dt_bias)


def _ssd_kernel(z_ref, x_ref, b_ref, c_ref, dt_ref, a_ref, dsk_ref, gg_ref, ex_ref, o_ref, st_ref):
    c = pl.program_id(1)
    q = SSM_CHUNK
    gw = x_ref.shape[1] // SSM_GROUPS
    hpg = gw // SSM_HEAD_DIM

    @pl.when(c == 0)
    def _():
        st_ref[...] = jnp.zeros_like(st_ref)

    li = lax.broadcasted_iota(I32, (q, q), 0)
    si = lax.broadcasted_iota(I32, (q, q), 1)
    causal = si <= li
    tril = jnp.where(causal, 1.0, 0.0).astype(BF16)

    dt = dt_ref[...]
    da = dt * a_ref[...]
    acum = _dot_hilo_rhs(tril, da)
    acum_t = acum.T
    ex = ex_ref[...]
    dt_x = _dot_hilo(dt, ex)
    acum_x = _dot_hilo(acum, ex)
    last_x = acum_x[q - 1:q, :]
    x = x_ref[...]
    xdt = x * dt_x
    grow = jnp.exp(acum_x)
    xw = (xdt * jnp.exp(last_x - acum_x)).astype(BF16)
    sdec = jnp.exp(last_x)
    xdt_b = xdt.astype(BF16)

    for g in range(SSM_GROUPS):
        gs = slice(g * gw, (g + 1) * gw)
        bg = b_ref[:, g * SSM_STATE:(g + 1) * SSM_STATE]
        cg = c_ref[:, g * SSM_STATE:(g + 1) * SSM_STATE].astype(BF16)
        cb = _dot_nt(cg, bg.astype(BF16))
        bg_t = bg.T.astype(BF16)
        yd = []
        for r in range(hpg):
            hh = g * hpg + r
            seg = acum[:, hh:hh + 1] - acum_t[hh:hh + 1, :]
            decay = jnp.exp(jnp.where(causal, seg, NEG_INF))
            m = (cb * decay).astype(BF16)
            yd.append(_dot(m, xdt_b[:, g * gw + r * SSM_HEAD_DIM:g * gw + (r + 1) * SSM_HEAD_DIM]))
        y = jnp.concatenate(yd, axis=1)
        st = st_ref[g]
        y = y + _dot(cg, st.astype(BF16)) * grow[:, gs]
        st_ref[g] = st * sdec[:, gs] + _dot(bg_t, xw[:, gs])
        y = y + dsk_ref[:, gs] * x[:, gs]
        zg = z_ref[:, gs]
        y = y * (zg * _sigmoid(zg))
        o_ref[:, gs] = _rms_rows(y, gg_ref[:, gs])


def _dot_hilo_rhs(a_bf16, b):
    hi = b.astype(BF16)
    lo = (b - hi.astype(F32)).astype(BF16)
    return _dot(a_bf16, hi) + _dot(a_bf16, lo)


def _ssd(zxbc, dt, a_row, dskip_x, gate_gain, expand, batch, inner):
    T = zxbc.shape[0]
    q = SSM_CHUNK
    nc = T // batch // q
    bw = SSM_GROUPS * SSM_STATE
    row = lambda col: (lambda b, c: (b * nc + c, col))
    full = lambda b, c: (0, 0)
    return pl.pallas_call(
        _ssd_kernel,
        grid=(batch, nc),
        in_specs=[
            pl.BlockSpec((q, inner), row(0)),
            pl.BlockSpec((q, inner), row(1)),
            pl.BlockSpec((q, bw), row(2 * inner // bw)),
            pl.BlockSpec((q, bw), row(2 * inner // bw + 1)),
            pl.BlockSpec((q, LANES), row(0)),
            pl.BlockSpec((1, LANES), full),
            pl.BlockSpec((1, inner), full),
            pl.BlockSpec((1, inner), full),
            pl.BlockSpec((LANES, inner), full),
        ],
        out_specs=pl.BlockSpec((q, inner), row(0)),
        out_shape=jax.ShapeDtypeStruct((T, inner), F32),
        scratch_shapes=[pltpu.VMEM((SSM_GROUPS, SSM_STATE, inner // SSM_GROUPS), F32)],
        compiler_params=_params("arbitrary", "arbitrary"),
        name="ssd",
    )(zxbc, zxbc, zxbc, zxbc, dt, a_row, dskip_x, gate_gain, expand)


def _proj_res_kernel(x_ref, w_ref, h_ref, o_ref):
    o_ref[...] = h_ref[...] + _dot(x_ref[...].astype(BF16), w_ref[...])


def _proj_res(x2, w, h2):
    T, K = x2.shape
    D = w.shape[1]
    tm, tn = ROW_TILE, COL_TILE
    return pl.pallas_call(
        _proj_res_kernel,
        grid=(T // tm, D // tn),
        in_specs=[
            pl.BlockSpec((tm, K), lambda i, j: (i, 0)),
            pl.BlockSpec((K, tn), lambda i, j: (0, j)),
            pl.BlockSpec((tm, tn), lambda i, j: (i, j)),
        ],
        out_specs=pl.BlockSpec((tm, tn), lambda i, j: (i, j)),
        out_shape=jax.ShapeDtypeStruct((T, D), F32),
        compiler_params=_params("arbitrary", "arbitrary"),
        name="proj_res",
    )(x2, w, h2)


def _peer_score_kernel(h_ref, g_ref, w_ref, k_ref, s_ref, xn_out_ref, xn_ref):
    j = pl.program_id(1)

    @pl.when(j == 0)
    def _():
        xn = _rms_rows(h_ref[...], g_ref[...])
        xn_out_ref[...] = xn
        xn_ref[...] = xn.astype(BF16)

    qv = _dot(xn_ref[...], w_ref[...])
    for cidx in range(qv.shape[1] // LANES):
        qc = qv[:, cidx * LANES:(cidx + 1) * LANES]
        hi = qc.astype(BF16)
        lo = (qc - hi.astype(F32)).astype(BF16)
        keys = k_ref[cidx % 2]
        s_ref[cidx] = _dot_nt(keys, hi) + _dot_nt(keys, lo)


def _peer_scores(h2, gain, w_q, keys):
    T, D = h2.shape
    N = w_q.shape[1]
    tm, tn = ROW_TILE, COL_TILE
    per = tn // LANES
    return pl.pallas_call(
        _peer_score_kernel,
        grid=(T // tm, N // tn),
        in_specs=[
            pl.BlockSpec((tm, D), lambda i, j: (i, 0)),
            pl.BlockSpec((1, D), lambda i, j: (0, 0)),
            pl.BlockSpec((D, tn), lambda i, j: (0, j)),
            pl.BlockSpec(keys.shape, lambda i, j: (0, 0, 0)),
        ],
        out_specs=[pl.BlockSpec((per, PEER_N_KEYS, tm), lambda i, j: (j, 0, i)),
                   pl.BlockSpec((tm, D), lambda i, j: (i, 0))],
        out_shape=[jax.ShapeDtypeStruct((N // LANES, PEER_N_KEYS, T), F32),
                   jax.ShapeDtypeStruct((T, D), F32)],
        scratch_shapes=[pltpu.VMEM((tm, D), BF16)],
        compiler_params=_params("arbitrary", "arbitrary"),
        name="peer_scores",
    )(h2, gain, w_q, keys)


PAIR_COUNTS = tuple(PEER_TOPK // (a + 1) for a in range(PEER_TOPK))
N_PAIRS = sum(PAIR_COUNTS)
PAIR_ROWS = -(-N_PAIRS // SUBLANES) * SUBLANES


def _first_max(v, rows):
    m = jnp.max(v, axis=0, keepdims=True)
    idx = jnp.min(jnp.where(v == m, rows, float(v.shape[0])), axis=0, keepdims=True)
    return m, idx


def _peer_topk_kernel(s_ref, rep_ref, idx_ref, gate_ref, s_scr, m_scr, i_scr, cs_scr, ci_scr,
                      bs_scr, be_scr):
    h = pl.program_id(1)
    tt = s_ref.shape[2]
    k = PEER_TOPK
    rows_k = lax.broadcasted_iota(I32, (PEER_N_KEYS, tt), 0).astype(F32)
    rows_p = lax.broadcasted_iota(I32, (PAIR_ROWS, tt), 0).astype(F32)

    s_scr[...] = s_ref[...]

    def stage1(a, carry):
        for half in range(2):
            v = s_scr[half]
            m, idx = _first_max(v, rows_k)
            s_scr[half] = jnp.where(rows_k == idx, NEG_INF, v)
            m_scr[half, pl.ds(a, 1), :] = m
            i_scr[half, pl.ds(a, 1), :] = idx
        return carry

    lax.fori_loop(0, k, stage1, 0)

    off = 0
    for a, nb in enumerate(PAIR_COUNTS):
        cs_scr[off:off + nb, :] = m_scr[0, a:a + 1, :] + m_scr[1, 0:nb, :]
        ci_scr[off:off + nb, :] = i_scr[0, a:a + 1, :] * float(PEER_N_KEYS) + i_scr[1, 0:nb, :]
        off += nb
    cs_scr[N_PAIRS:PAIR_ROWS, :] = jnp.full((PAIR_ROWS - N_PAIRS, tt), NEG_INF, F32)
    ci_scr[N_PAIRS:PAIR_ROWS, :] = jnp.zeros((PAIR_ROWS - N_PAIRS, tt), F32)
    ci = ci_scr[...]

    def stage2(n, carry):
        cs = cs_scr[...]
        m, pos = _first_max(cs, rows_p)
        hit = rows_p == pos
        cs_scr[...] = jnp.where(hit, NEG_INF, cs)
        bs_scr[pl.ds(h * k + n, 1), :] = m
        be_scr[pl.ds(h * k + n, 1), :] = jnp.max(jnp.where(hit, ci, -1.0), axis=0, keepdims=True)
        return carry

    lax.fori_loop(0, k, stage2, 0)

    @pl.when(h == PEER_HEADS - 1)
    def _():
        gates = []
        for hh in range(PEER_HEADS):
            blk = bs_scr[hh * k:(hh + 1) * k, :]
            p = jnp.exp(blk - jnp.max(blk, axis=0, keepdims=True))
            gates.append(p / jnp.sum(p, axis=0, keepdims=True))
        gate = jnp.concatenate(gates, axis=0).T
        gate_ref[...] = _dot_hilo(gate, rep_ref[...])
        idx_ref[...] = (be_scr[...].T * float(WORD_ROWS)).astype(I32)


def _peer_topk(scores_t, rep):
    n_chunks, n_keys, T = scores_t.shape
    tt = 256
    return pl.pallas_call(
        _peer_topk_kernel,
        grid=(T // tt, PEER_HEADS),
        in_specs=[pl.BlockSpec((2, n_keys, tt), lambda i, h: (h, 0, i)),
                  pl.BlockSpec(rep.shape, lambda i, h: (0, 0))],
        out_specs=[pl.BlockSpec((tt, PEER_SEL), lambda i, h: (i, 0)),
                   pl.BlockSpec((tt, PEER_SEL * SUBLANES), lambda i, h: (i, 0))],
        out_shape=[jax.ShapeDtypeStruct((T, PEER_SEL), I32),
                   jax.ShapeDtypeStruct((T, PEER_SEL * SUBLANES), F32)],
        scratch_shapes=[pltpu.VMEM((2, n_keys, tt), F32),
                        pltpu.VMEM((2, PEER_TOPK, tt), F32), pltpu.VMEM((2, PEER_TOPK, tt), F32),
                        pltpu.VMEM((PAIR_ROWS, tt), F32), pltpu.VMEM((PAIR_ROWS, tt), F32),
                        pltpu.VMEM((PEER_SEL, tt), F32), pltpu.VMEM((PEER_SEL, tt), F32)],
        compiler_params=_params("arbitrary", "arbitrary"),
        name="peer_topk",
    )(scores_t, rep)


def _gather_experts(idx_ref, t, tab_ref, stage_ref):
    for kk in range(PEER_SEL):
        r0 = pl.multiple_of(idx_ref[t, kk], WORD_ROWS)
        stage_ref[kk * WORD_ROWS:(kk + 1) * WORD_ROWS, :] = tab_ref[pl.ds(r0, WORD_ROWS), :]


def _staged(stage_ref):
    return pltpu.bitcast(stage_ref[...], BF16)


def _idx_copy(idx_hbm, idx_smem, sems, half_block, slot):
    start = pl.multiple_of(half_block * IDX_HALF, IDX_HALF)
    return pltpu.make_async_copy(idx_hbm.at[pl.ds(start, IDX_HALF)], idx_smem.at[slot], sems.at[slot])


def _for_each_token(idx_hbm, idx_smem, sems, tab_ref, stage_a, stage_b, compute):
    i = pl.program_id(0)
    n = pl.num_programs(0)
    stages = (stage_a, stage_b)

    @pl.when(i == 0)
    def _():
        _idx_copy(idx_hbm, idx_smem, sems, 0, 0).start()

    for slot in range(2):
        _idx_copy(idx_hbm, idx_smem, sems, 2 * i + slot, slot).wait()
        if slot == 0:
            _idx_copy(idx_hbm, idx_smem, sems, 2 * i + 1, 1).start()
        else:
            @pl.when(i + 1 < n)
            def _():
                _idx_copy(idx_hbm, idx_smem, sems, 2 * i + 2, 0).start()
        rows = idx_smem.at[slot]
        _gather_experts(rows, 0, tab_ref, stages[0])
        for t in range(IDX_HALF):
            if t + 1 < IDX_HALF:
                _gather_experts(rows, t + 1, tab_ref, stages[(t + 1) % 2])
            compute(slot * IDX_HALF + t, _staged(stages[t % 2]))


def _diag_mask():
    n = PEER_SEL * SUBLANES
    row = lax.broadcasted_iota(I32, (SUBLANES, n), 0)
    col = lax.broadcasted_iota(I32, (SUBLANES, n), 1)
    return (col % SUBLANES == row).astype(F32)


def _peer_down_kernel(idx_hbm, x_ref, gate_ref, tab_ref, sel_ref, coef_ref, stage_a, stage_b, r_ref,
                      idx_smem, sems):
    mask = _diag_mask()

    def compute(t, w):
        x_tile = x_ref[pl.ds(t, 1), :].reshape(SUBLANES, LANES)
        y = _dot_nt(x_tile.astype(BF16), w)
        r_ref[pl.ds(t, 1), :] = jnp.sum(y * mask, axis=0, keepdims=True)

    _for_each_token(idx_hbm, idx_smem, sems, tab_ref, stage_a, stage_b, compute)
    act = _dot_hilo(r_ref[...], sel_ref[...])
    gelu = 0.5 * act * (1.0 + lax.erf(act * (2.0 ** -0.5)))
    coef_ref[...] = gate_ref[...] * gelu


def _peer_down(idx, xn, gate_rep, tab, sel):
    T = idx.shape[0]
    tb = 2 * IDX_HALF
    n = PEER_SEL * SUBLANES
    return pl.pallas_call(
        _peer_down_kernel,
        grid=(T // tb,),
        in_specs=[
            pl.BlockSpec(memory_space=pl.ANY),
            pl.BlockSpec((tb, n), lambda i: (i, 0)),
            pl.BlockSpec((tb, n), lambda i: (i, 0)),
            pl.BlockSpec(memory_space=pltpu.VMEM),
            pl.BlockSpec(memory_space=pltpu.VMEM),
        ],
        out_specs=pl.BlockSpec((tb, n), lambda i: (i, 0)),
        out_shape=jax.ShapeDtypeStruct((T, n), F32),
        scratch_shapes=[pltpu.VMEM((PEER_SEL * WORD_ROWS, LANES), jnp.uint32), pltpu.VMEM((PEER_SEL * WORD_ROWS, LANES), jnp.uint32),
                        pltpu.VMEM((tb, n), F32), pltpu.SMEM((2, IDX_HALF, PEER_SEL), I32), pltpu.SemaphoreType.DMA((2,))],
        compiler_params=_params("arbitrary"),
        name="peer_down",
    )(idx, xn, gate_rep, tab, sel)


def _peer_up_kernel(idx_hbm, coef_ref, h_ref, tab_ref, o_ref, stage_a, stage_b, idx_smem, sems):
    mask = _diag_mask()

    def compute(t, w):
        cm = (coef_ref[pl.ds(t, 1), :] * mask).astype(BF16)
        o_tile = h_ref[pl.ds(t, 1), :].reshape(SUBLANES, LANES) + _dot(cm, w)
        o_ref[pl.ds(t, 1), :] = o_tile.reshape(1, SUBLANES * LANES)

    _for_each_token(idx_hbm, idx_smem, sems, tab_ref, stage_a, stage_b, compute)


def _peer_up(idx, coef_rep, h2, tab):
    T = idx.shape[0]
    tb = 2 * IDX_HALF
    n = PEER_SEL * SUBLANES
    return pl.pallas_call(
        _peer_up_kernel,
        grid=(T // tb,),
        in_specs=[
            pl.BlockSpec(memory_space=pl.ANY),
            pl.BlockSpec((tb, n), lambda i: (i, 0)),
            pl.BlockSpec((tb, n), lambda i: (i, 0)),
            pl.BlockSpec(memory_space=pltpu.VMEM),
        ],
        out_specs=pl.BlockSpec((tb, n), lambda i: (i, 0)),
        out_shape=jax.ShapeDtypeStruct((T, n), F32),
        scratch_shapes=[pltpu.VMEM((PEER_SEL * WORD_ROWS, LANES), jnp.uint32), pltpu.VMEM((PEER_SEL * WORD_ROWS, LANES), jnp.uint32),
                        pltpu.SMEM((2, IDX_HALF, PEER_SEL), I32), pltpu.SemaphoreType.DMA((2,))],
        compiler_params=_params("arbitrary"),
        name="peer_up",
    )(idx, coef_rep, h2, tab)


def _pack_table(tab):
    e, d = tab.shape
    t = tab.astype(BF16).reshape(e, WORD_ROWS, 2, LANES)
    t = jnp.swapaxes(t, 2, 3)
    return lax.bitcast_convert_type(t, jnp.uint32).reshape(e * WORD_ROWS, LANES)


def _peer_layer(h2, gain, w_q, sub_keys, tab_u, tab_v):
    T, D = h2.shape
    n_rep = PEER_SEL * SUBLANES
    rep_lane = jnp.arange(n_rep)
    rep = (jnp.arange(PEER_SEL)[:, None] == rep_lane[None, :] // SUBLANES).astype(BF16)
    sel = (rep_lane[:, None] // SUBLANES == rep_lane[None, :] // SUBLANES).astype(BF16)
    scores, xn = _peer_scores(h2, gain.reshape(1, D), w_q.astype(BF16), sub_keys.astype(BF16))
    idx, gate_rep = _peer_topk(scores, rep)
    coef_rep = _peer_down(idx, xn, gate_rep, _pack_table(tab_u), sel)
    return _peer_up(idx, coef_rep, h2, _pack_table(tab_v))


def _even_layer(h2, gain, w_in, pool_w, pool_scale, q_gain, k_gain, w_out, batch, seq_len):
    T, D = h2.shape
    aw = ATTN_HEADS * ATTN_HEAD_DIM
    lane = jnp.arange(aw)
    bd = ((lane[:, None] // ATTN_HEAD_DIM == lane[None, :] // ATTN_HEAD_DIM)
          .astype(F32) / ATTN_HEAD_DIM).astype(BF16)
    qk_gain = jnp.stack([jnp.tile(q_gain, ATTN_HEADS), jnp.tile(k_gain, ATTN_HEADS)])
    proj = _even_in(h2, gain, w_in.astype(BF16), pool_w.astype(BF16),
                    pool_scale.reshape(1, -1), qk_gain, bd, seq_len)
    proj3 = proj.reshape(batch, seq_len, -1)
    outs, lses = [], []
    for _, dilation in DILATED_BRANCHES:
        o, lse = _attn_branch(proj3, dilation)
        outs.append(o.reshape(T, aw))
        lses.append(lse.reshape(T, aw))
    return _even_out(proj, outs, lses, w_out.astype(BF16), h2)


def _ssm_layer(h2, gain, w_in, conv_w, conv_b, dt_bias, a_log, d_skip, gate_gain, w_out,
               batch, seq_len):
    inner = w_out.shape[0]
    n_heads = inner // SSM_HEAD_DIM
    main_w = 2 * inner + 2 * SSM_GROUPS * SSM_STATE
    zxbc = _ssm_in(h2, gain, w_in[:, :main_w].astype(BF16), conv_w, conv_b.reshape(1, -1),
                   seq_len, inner)
    w_dt = jnp.pad(w_in[:, main_w:], ((0, 0), (0, LANES - n_heads))).astype(BF16)
    b_dt = jnp.pad(dt_bias, (0, LANES - n_heads)).reshape(1, LANES)
    dt = _dt_proj(h2, gain, w_dt, b_dt)
    a_row = jnp.pad(-jnp.exp(a_log), (0, LANES - n_heads)).reshape(1, LANES)
    dskip_x = jnp.repeat(d_skip, SSM_HEAD_DIM).reshape(1, inner)
    expand = (jnp.arange(LANES)[:, None] == jnp.arange(inner)[None, :] // SSM_HEAD_DIM).astype(BF16)
    y = _ssd(zxbc, dt, a_row, dskip_x, gate_gain.reshape(1, inner), expand, batch, inner)
    return _proj_res(y, w_out.astype(BF16), h2)


def kernel(x, attn_norm, ffn_norm, even_w_in, pool_w, pool_scale, q_gain, k_gain, even_w_out, ssm_w_in, conv_w, conv_b, dt_bias, a_log, d_skip, gate_gain, ssm_w_out, peer_w_q, peer_sub_keys, peer_u, peer_v):
    B, S, D = x.shape
    T = B * S
    assert D == SUBLANES * LANES and S % ROW_TILE == 0
    assert all(S % (d * ATTN_BLOCK) == 0 for _, d in DILATED_BRANCHES)
    assert all(w // d == ATTN_BLOCK for w, d in DILATED_BRANCHES)
    h = x.reshape(T, D)
    for layer in range(attn_norm.shape[0]):
        i = layer // 2
        gain = attn_norm[layer].reshape(1, D)
        if layer % 2 == 0:
            h = _even_layer(h, gain, even_w_in[i], pool_w[i], pool_scale[i], q_gain[i], k_gain[i],
                            even_w_out[i], B, S)
        else:
            h = _ssm_layer(h, gain, ssm_w_in[i], conv_w[i], conv_b[i], dt_bias[i], a_log[i],
                           d_skip[i], gate_gain[i], ssm_w_out[i], B, S)
        h = _peer_layer(h, ffn_norm[layer], peer_w_q[layer], peer_sub_keys[layer],
                        peer_u[layer], peer_v[layer])
    return h.reshape(B, S, D)
```

```python
import functools
import math

import jax
import jax.numpy as jnp
from jax import lax
from jax.experimental import pallas as pl
from jax.experimental.pallas import tpu as pltpu

F32 = jnp.float32
BF16 = jnp.bfloat16
I32 = jnp.int32
EPS = 1e-6
NEG_INF = float("-inf")

LANES = 128
SUBLANES = 8
VMEM_LIMIT = 56 * 1024 * 1024

POOL_WINDOWS = (2, 4, 8, 16)
POOL_HALO = 16
ATTN_HEADS = 8
ATTN_HEAD_DIM = 64
ATTN_BLOCK = 128
DILATED_BRANCHES = ((128, 1), (512, 4), (2048, 16))
SSM_HEAD_DIM = 64
SSM_GROUPS = 8
SSM_STATE = 128
SSM_CONV = 4
SSM_CHUNK = 128
CONV_HALO = 8
PEER_HEADS = 8
PEER_N_KEYS = 128
PEER_TOPK = 16
PEER_SEL = PEER_HEADS * PEER_TOPK
WORD_ROWS = 4
IDX_HALF = 16
ROW_TILE = 512
COL_TILE = 512


def _params(*sem):
    return pltpu.CompilerParams(dimension_semantics=sem, vmem_limit_bytes=VMEM_LIMIT)


def _dot(a, b):
    return jnp.dot(a, b, preferred_element_type=F32)


def _dot_nt(a, b):
    return lax.dot_general(a, b, (((1,), (1,)), ((), ())), preferred_element_type=F32)


def _dot_hilo(a, b_bf16):
    hi = a.astype(BF16)
    lo = (a - hi.astype(F32)).astype(BF16)
    return _dot(hi, b_bf16) + _dot(lo, b_bf16)


def _rms_rows(x, gain):
    ms = jnp.mean(x * x, axis=-1, keepdims=True)
    return x * lax.rsqrt(ms + EPS) * gain


def _sigmoid(x):
    return 1.0 / (1.0 + jnp.exp(-x))


def _window_sum(ext, w, rows):
    e = ext
    span = 1
    while span < w:
        n = e.shape[0]
        e = e[span:n, :] + e[0:n - span, :]
        span *= 2
    start = POOL_HALO + 1 - w
    return e[start:start + rows, :]


def _even_in_kernel(seq_len, h_ref, g_ref, w_ref, pw_ref, ps_ref, qkg_ref, bd_ref, o_ref,
                    xn_ref, carry_ref):
    i = pl.program_id(0)
    j = pl.program_id(1)
    tm = h_ref.shape[0]

    @pl.when(j == 0)
    def _():
        xn_ref[...] = _rms_rows(h_ref[...], g_ref[...]).astype(BF16)

    acc = _dot(xn_ref[...], w_ref[...])

    @pl.when(j == 0)
    def _():
        pos0 = (i * tm) % seq_len

        @pl.when(pos0 == 0)
        def _():
            carry_ref[...] = jnp.zeros_like(carry_ref)

        ext = jnp.concatenate([carry_ref[...], acc], axis=0)
        pos = pos0 + lax.broadcasted_iota(I32, (tm, 1), 0)
        outs = []
        for gi, w in enumerate(POOL_WINDOWS):
            sl = slice(gi * LANES, (gi + 1) * LANES)
            ws = _window_sum(ext[:, sl], w, tm)
            cnt = jnp.minimum(pos + 1, w).astype(F32)
            mixed = ws / cnt - acc[:, sl]
            outs.append(_dot(mixed.astype(BF16), pw_ref[gi]))
        o_ref[...] = jnp.concatenate(outs, axis=1) * ps_ref[...]
        carry_ref[...] = acc[tm - POOL_HALO:tm, :]

    def qk_norm(row):
        ms = _dot_hilo(acc * acc, bd_ref[...])
        o_ref[...] = acc * lax.rsqrt(ms + EPS) * qkg_ref[row:row + 1, :]

    @pl.when(j == 1)
    def _():
        qk_norm(0)

    @pl.when(j == 2)
    def _():
        qk_norm(1)

    @pl.when(j == 3)
    def _():
        o_ref[...] = acc


def _even_in(h2, gain, w_in, pool_w, pool_scale, qk_gain, bd, seq_len):
    T, D = h2.shape
    N = w_in.shape[1]
    tm, tn = ROW_TILE, COL_TILE
    return pl.pallas_call(
        functools.partial(_even_in_kernel, seq_len),
        grid=(T // tm, N // tn),
        in_specs=[
            pl.BlockSpec((tm, D), lambda i, j: (i, 0)),
            pl.BlockSpec((1, D), lambda i, j: (0, 0)),
            pl.BlockSpec((D, tn), lambda i, j: (0, j)),
            pl.BlockSpec(pool_w.shape, lambda i, j: (0, 0, 0)),
            pl.BlockSpec((1, tn), lambda i, j: (0, 0)),
            pl.BlockSpec((2, tn), lambda i, j: (0, 0)),
            pl.BlockSpec((tn, tn), lambda i, j: (0, 0)),
        ],
        out_specs=pl.BlockSpec((tm, tn), lambda i, j: (i, j)),
        out_shape=jax.ShapeDtypeStruct((T, N), F32),
        scratch_shapes=[pltpu.VMEM((tm, D), BF16), pltpu.VMEM((POOL_HALO, tn), F32)],
        compiler_params=_params("arbitrary", "arbitrary"),
        name="even_in",
    )(h2, gain, w_in, pool_w, pool_scale, qk_gain, bd)


def _attn_kernel(dilation, q_ref, kp_ref, kc_ref, vp_ref, vc_ref, o_ref, lse_ref):
    n = pl.program_id(2)
    qb = ATTN_BLOCK
    qi = lax.broadcasted_iota(I32, (qb, qb), 0)
    kj = lax.broadcasted_iota(I32, (qb, qb), 1)
    lane = lax.broadcasted_iota(I32, (qb, LANES), 1)
    valid_c = kj <= qi
    valid_p = (kj >= qi) & (n > 0)
    dist_c = ((qi - kj) * dilation).astype(F32)
    dist_p = ((qi + qb - kj) * dilation).astype(F32)
    scale = ATTN_HEAD_DIM ** -0.5
    heads_per_tile = LANES // ATTN_HEAD_DIM
    for hp in range(ATTN_HEADS // heads_per_tile):
        sl = slice(hp * LANES, (hp + 1) * LANES)
        qp = q_ref[:, sl]
        kc = kc_ref[:, sl].astype(BF16)
        kp = kp_ref[:, sl].astype(BF16)
        vc = vc_ref[:, sl].astype(BF16)
        vp = vp_ref[:, sl].astype(BF16)
        o_tile = jnp.zeros((qb, LANES), F32)
        lse_tile = jnp.zeros((qb, LANES), F32)
        for e in range(heads_per_tile):
            head = hp * heads_per_tile + e
            slope = 2.0 ** (-8.0 * (head + 1) / ATTN_HEADS)
            in_head = (lane >= e * ATTN_HEAD_DIM) & (lane < (e + 1) * ATTN_HEAD_DIM)
            qm = jnp.where(in_head, qp, 0.0).astype(BF16)
            s_c = _dot_nt(qm, kc) * scale - slope * dist_c
            s_p = _dot_nt(qm, kp) * scale - slope * dist_p
            s_c = jnp.where(valid_c, s_c, NEG_INF)
            s_p = jnp.where(valid_p, s_p, NEG_INF)
            m = jnp.maximum(jnp.max(s_c, axis=-1, keepdims=True),
                            jnp.max(s_p, axis=-1, keepdims=True))
            p_c = jnp.exp(s_c - m)
            p_p = jnp.exp(s_p - m)
            l = jnp.sum(p_c, axis=-1, keepdims=True) + jnp.sum(p_p, axis=-1, keepdims=True)
            o = (_dot(p_c.astype(BF16), vc) + _dot(p_p.astype(BF16), vp)) / l
            lse = m + jnp.log(l)
            o_tile = jnp.where(in_head, o, o_tile)
            lse_tile = jnp.where(in_head, lse, lse_tile)
        o_ref[:, sl] = o_tile
        lse_ref[:, sl] = lse_tile


def _attn_branch(proj, dilation):
    B, S, W = proj.shape
    aw = ATTN_HEADS * ATTN_HEAD_DIM
    L = S // dilation
    nb = L // ATTN_BLOCK
    cols = W // aw
    pv = proj.reshape(B, L, dilation * W)
    blk = (None, ATTN_BLOCK, aw)

    def spec(col, prev):
        if prev:
            return pl.BlockSpec(blk, lambda b, r, n: (b, jnp.maximum(n - 1, 0), r * cols + col))
        return pl.BlockSpec(blk, lambda b, r, n: (b, n, r * cols + col))

    out_spec = pl.BlockSpec(blk, lambda b, r, n: (b, n, r))
    o, lse = pl.pallas_call(
        functools.partial(_attn_kernel, dilation),
        grid=(B, dilation, nb),
        in_specs=[spec(1, False), spec(2, True), spec(2, False), spec(3, True), spec(3, False)],
        out_specs=[out_spec, out_spec],
        out_shape=[jax.ShapeDtypeStruct((B, L, dilation * aw), F32)] * 2,
        compiler_params=_params("arbitrary", "arbitrary", "arbitrary"),
        name="attn",
    )(pv, pv, pv, pv, pv)
    return o.reshape(B, S, aw), lse.reshape(B, S, aw)


def _even_out_kernel(a_ref, o1, o2, o3, l1, l2, l3, wa_ref, wo_ref, h_ref, out_ref, mix_ref):
    j = pl.program_id(1)
    half = a_ref.shape[1]

    @pl.when(j == 0)
    def _():
        la, lb, lc = l1[...], l2[...], l3[...]
        mx = jnp.maximum(jnp.maximum(la, lb), lc)
        wa, wb, wc = jnp.exp(la - mx), jnp.exp(lb - mx), jnp.exp(lc - mx)
        den = wa + wb + wc
        o = (wa / den) * o1[...] + (wb / den) * o2[...] + (wc / den) * o3[...]
        mix_ref[:, 0:half] = a_ref[...].astype(BF16)
        mix_ref[:, half:2 * half] = o.astype(BF16)

    out_ref[...] = (h_ref[...] + _dot(mix_ref[:, 0:half], wa_ref[...])
                    + _dot(mix_ref[:, half:2 * half], wo_ref[...]))


def _even_out(proj2, os_, lses, w_out, h2):
    T, D = h2.shape
    half = w_out.shape[0] // 2
    tm, tn = ROW_TILE, COL_TILE
    row = pl.BlockSpec((tm, half), lambda i, j: (i, 0))
    return pl.pallas_call(
        _even_out_kernel,
        grid=(T // tm, D // tn),
        in_specs=[row] * 7 + [
            pl.BlockSpec((half, tn), lambda i, j: (0, j)),
            pl.BlockSpec((half, tn), lambda i, j: (1, j)),
            pl.BlockSpec((tm, tn), lambda i, j: (i, j)),
        ],
        out_specs=pl.BlockSpec((tm, tn), lambda i, j: (i, j)),
        out_shape=jax.ShapeDtypeStruct((T, D), F32),
        scratch_shapes=[pltpu.VMEM((tm, 2 * half), BF16)],
        compiler_params=_params("arbitrary", "arbitrary"),
        name="even_out",
    )(proj2, *os_, *lses, w_out, w_out, h2)


def _ssm_in_kernel(seq_len, n_plain, h_ref, g_ref, w_ref, cw_ref, cb_ref, o_ref, xn_ref, carry_ref):
    i = pl.program_id(0)
    j = pl.program_id(1)
    tm = h_ref.shape[0]

    @pl.when(j == 0)
    def _():
        xn_ref[...] = _rms_rows(h_ref[...], g_ref[...]).astype(BF16)

    acc = _dot(xn_ref[...], w_ref[...])

    @pl.when(j < n_plain)
    def _():
        o_ref[...] = acc

    @pl.when(j >= n_plain)
    def _():
        jc = j - n_plain
        pos0 = (i * tm) % seq_len

        @pl.when(pos0 == 0)
        def _():
            carry_ref[jc] = jnp.zeros(carry_ref.shape[1:], F32)

        ext = jnp.concatenate([carry_ref[jc], acc], axis=0)
        y = cb_ref[...]
        for k in range(SSM_CONV):
            off = CONV_HALO - (SSM_CONV - 1) + k
            y = y + cw_ref[k:k + 1, :] * ext[off:off + tm, :]
        o_ref[...] = y * _sigmoid(y)
        carry_ref[jc] = acc[tm - CONV_HALO:tm, :]


def _ssm_in(h2, gain, w_main, conv_w, conv_b, seq_len, inner):
    T, D = h2.shape
    N = w_main.shape[1]
    tm, tn = ROW_TILE, COL_TILE
    n_plain = inner // tn
    n_conv = N // tn - n_plain
    return pl.pallas_call(
        functools.partial(_ssm_in_kernel, seq_len, n_plain),
        grid=(T // tm, N // tn),
        in_specs=[
            pl.BlockSpec((tm, D), lambda i, j: (i, 0)),
            pl.BlockSpec((1, D), lambda i, j: (0, 0)),
            pl.BlockSpec((D, tn), lambda i, j: (0, j)),
            pl.BlockSpec((SSM_CONV, tn), lambda i, j: (0, jnp.maximum(j - n_plain, 0))),
            pl.BlockSpec((1, tn), lambda i, j: (0, jnp.maximum(j - n_plain, 0))),
        ],
        out_specs=pl.BlockSpec((tm, tn), lambda i, j: (i, j)),
        out_shape=jax.ShapeDtypeStruct((T, N), F32),
        scratch_shapes=[pltpu.VMEM((tm, D), BF16), pltpu.VMEM((n_conv, CONV_HALO, tn), F32)],
        compiler_params=_params("arbitrary", "arbitrary"),
        name="ssm_in",
    )(h2, gain, w_main, conv_w, conv_b)


def _dt_kernel(h_ref, g_ref, w_ref, b_ref, o_ref):
    xn = _rms_rows(h_ref[...], g_ref[...]).astype(BF16)
    raw = _dot(xn, w_ref[...]) + b_ref[...]
    o_ref[...] = jnp.maximum(raw, 0.0) + jnp.log(1.0 + jnp.exp(-jnp.abs(raw)))


def _dt_proj(h2, gain, w_dt, dt_bias):
    T, D = h2.shape
    tm = ROW_TILE
    return pl.pallas_call(
        _dt_kernel,
        grid=(T // tm,),
        in_specs=[
            pl.BlockSpec((tm, D), lambda i: (i, 0)),
            pl.BlockSpec((1, D), lambda i: (0, 0)),
            pl.BlockSpec((D, LANES), lambda i: (0, 0)),
            pl.BlockSpec((1, LANES), lambda i: (0, 0)),
        ],
        out_specs=pl.BlockSpec((tm, LANES), lambda i: (i, 0)),
        out_shape=jax.ShapeDtypeStruct((T, LANES), F32),
        compiler_params=_params("arbitrary"),
        name="dt",
    )(h2, gain, w_dt, dt_bias)


def _ssd_kernel(z_ref, x_ref, b_ref, c_ref, dt_ref, a_ref, dsk_ref, gg_ref, ex_ref, o_ref, st_ref):
    c = pl.program_id(1)
    q = SSM_CHUNK
    gw = x_ref.shape[1] // SSM_GROUPS
    hpg = gw // SSM_HEAD_DIM

    @pl.when(c == 0)
    def _():
        st_ref[...] = jnp.zeros_like(st_ref)

    li = lax.broadcasted_iota(I32, (q, q), 0)
    si = lax.broadcasted_iota(I32, (q, q), 1)
    causal = si <= li
    tril = jnp.where(causal, 1.0, 0.0).astype(BF16)

    dt = dt_ref[...]
    da = dt * a_ref[...]
    acum = _dot_hilo_rhs(tril, da)
    acum_t = acum.T
    ex = ex_ref[...]
    dt_x = _dot_hilo(dt, ex)
    acum_x = _dot_hilo(acum, ex)
    last_x = acum_x[q - 1:q, :]
    x = x_ref[...]
    xdt = x * dt_x
    grow = jnp.exp(acum_x)
    xw = (xdt * jnp.exp(last_x - acum_x)).astype(BF16)
    sdec = jnp.exp(last_x)
    xdt_b = xdt.astype(BF16)

    for g in range(SSM_GROUPS):
        gs = slice(g * gw, (g + 1) * gw)
        bg = b_ref[:, g * SSM_STATE:(g + 1) * SSM_STATE]
        cg = c_ref[:, g * SSM_STATE:(g + 1) * SSM_STATE].astype(BF16)
        cb = _dot_nt(cg, bg.astype(BF16))
        bg_t = bg.T.astype(BF16)
        yd = []
        for r in range(hpg):
            hh = g * hpg + r
            seg = acum[:, hh:hh + 1] - acum_t[hh:hh + 1, :]
            decay = jnp.exp(jnp.where(causal, seg, NEG_INF))
            m = (cb * decay).astype(BF16)
            yd.append(_dot(m, xdt_b[:, g * gw + r * SSM_HEAD_DIM:g * gw + (r + 1) * SSM_HEAD_DIM]))
        y = jnp.concatenate(yd, axis=1)
        st = st_ref[g]
        y = y + _dot(cg, st.astype(BF16)) * grow[:, gs]
        st_ref[g] = st * sdec[:, gs] + _dot(bg_t, xw[:, gs])
        y = y + dsk_ref[:, gs] * x[:, gs]
        zg = z_ref[:, gs]
        y = y * (zg * _sigmoid(zg))
        o_ref[:, gs] = _rms_rows(y, gg_ref[:, gs])


def _dot_hilo_rhs(a_bf16, b):
    hi = b.astype(BF16)
    lo = (b - hi.astype(F32)).astype(BF16)
    return _dot(a_bf16, hi) + _dot(a_bf16, lo)


def _ssd(zxbc, dt, a_row, dskip_x, gate_gain, expand, batch, inner):
    T = zxbc.shape[0]
    q = SSM_CHUNK
    nc = T // batch // q
    bw = SSM_GROUPS * SSM_STATE
    row = lambda col: (lambda b, c: (b * nc + c, col))
    full = lambda b, c: (0, 0)
    return pl.pallas_call(
        _ssd_kernel,
        grid=(batch, nc),
        in_specs=[
            pl.BlockSpec((q, inner), row(0)),
            pl.BlockSpec((q, inner), row(1)),
            pl.BlockSpec((q, bw), row(2 * inner // bw)),
            pl.BlockSpec((q, bw), row(2 * inner // bw + 1)),
            pl.BlockSpec((q, LANES), row(0)),
            pl.BlockSpec((1, LANES), full),
            pl.BlockSpec((1, inner), full),
            pl.BlockSpec((1, inner), full),
            pl.BlockSpec((LANES, inner), full),
        ],
        out_specs=pl.BlockSpec((q, inner), row(0)),
        out_shape=jax.ShapeDtypeStruct((T, inner), F32),
        scratch_shapes=[pltpu.VMEM((SSM_GROUPS, SSM_STATE, inner // SSM_GROUPS), F32)],
        compiler_params=_params("arbitrary", "arbitrary"),
        name="ssd",
    )(zxbc, zxbc, zxbc, zxbc, dt, a_row, dskip_x, gate_gain, expand)


def _proj_res_kernel(x_ref, w_ref, h_ref, o_ref):
    o_ref[...] = h_ref[...] + _dot(x_ref[...].astype(BF16), w_ref[...])


def _proj_res(x2, w, h2):
    T, K = x2.shape
    D = w.shape[1]
    tm, tn = ROW_TILE, COL_TILE
    return pl.pallas_call(
        _proj_res_kernel,
        grid=(T // tm, D // tn),
        in_specs=[
            pl.BlockSpec((tm, K), lambda i, j: (i, 0)),
            pl.BlockSpec((K, tn), lambda i, j: (0, j)),
            pl.BlockSpec((tm, tn), lambda i, j: (i, j)),
        ],
        out_specs=pl.BlockSpec((tm, tn), lambda i, j: (i, j)),
        out_shape=jax.ShapeDtypeStruct((T, D), F32),
        compiler_params=_params("arbitrary", "arbitrary"),
        name="proj_res",
    )(x2, w, h2)


def _peer_score_kernel(h_ref, g_ref, w_ref, k_ref, s_ref, xn_out_ref, xn_ref):
    j = pl.program_id(1)

    @pl.when(j == 0)
    def _():
        xn = _rms_rows(h_ref[...], g_ref[...])
        xn_out_ref[...] = xn
        xn_ref[...] = xn.astype(BF16)

    qv = _dot(xn_ref[...], w_ref[...])
    for cidx in range(qv.shape[1] // LANES):
        qc = qv[:, cidx * LANES:(cidx + 1) * LANES]
        hi = qc.astype(BF16)
        lo = (qc - hi.astype(F32)).astype(BF16)
        keys = k_ref[cidx % 2]
        s_ref[cidx] = _dot_nt(keys, hi) + _dot_nt(keys, lo)


def _peer_scores(h2, gain, w_q, keys):
    T, D = h2.shape
    N = w_q.shape[1]
    tm, tn = ROW_TILE, COL_TILE
    per = tn // LANES
    return pl.pallas_call(
        _peer_score_kernel,
        grid=(T // tm, N // tn),
        in_specs=[
            pl.BlockSpec((tm, D), lambda i, j: (i, 0)),
            pl.BlockSpec((1, D), lambda i, j: (0, 0)),
            pl.BlockSpec((D, tn), lambda i, j: (0, j)),
            pl.BlockSpec(keys.shape, lambda i, j: (0, 0, 0)),
        ],
        out_specs=[pl.BlockSpec((per, PEER_N_KEYS, tm), lambda i, j: (j, 0, i)),
                   pl.BlockSpec((tm, D), lambda i, j: (i, 0))],
        out_shape=[jax.ShapeDtypeStruct((N // LANES, PEER_N_KEYS, T), F32),
                   jax.ShapeDtypeStruct((T, D), F32)],
        scratch_shapes=[pltpu.VMEM((tm, D), BF16)],
        compiler_params=_params("arbitrary", "arbitrary"),
        name="peer_scores",
    )(h2, gain, w_q, keys)


PAIR_COUNTS = tuple(PEER_TOPK // (a + 1) for a in range(PEER_TOPK))
N_PAIRS = sum(PAIR_COUNTS)
PAIR_ROWS = -(-N_PAIRS // SUBLANES) * SUBLANES


def _first_max(v, rows):
    m = jnp.max(v, axis=0, keepdims=True)
    idx = jnp.min(jnp.where(v == m, rows, float(v.shape[0])), axis=0, keepdims=True)
    return m, idx


def _peer_topk_kernel(s_ref, idx_ref, gate_ref, s_scr, m_scr, i_scr, cs_scr, ci_scr,
                      bs_scr, be_scr):
    h = pl.program_id(1)
    tt = s_ref.shape[2]
    k = PEER_TOPK
    rows_k = lax.broadcasted_iota(I32, (PEER_N_KEYS, tt), 0).astype(F32)
    rows_p = lax.broadcasted_iota(I32, (PAIR_ROWS, tt), 0).astype(F32)

    s_scr[...] = s_ref[...]

    def stage1(a, carry):
        for half in range(2):
            v = s_scr[half]
            m, idx = _first_max(v, rows_k)
            s_scr[half] = jnp.where(rows_k == idx, NEG_INF, v)
            m_scr[half, pl.ds(a, 1), :] = m
            i_scr[half, pl.ds(a, 1), :] = idx
        return carry

    lax.fori_loop(0, k, stage1, 0)

    off = 0
    for a, nb in enumerate(PAIR_COUNTS):
        cs_scr[off:off + nb, :] = m_scr[0, a:a + 1, :] + m_scr[1, 0:nb, :]
        ci_scr[off:off + nb, :] = i_scr[0, a:a + 1, :] * float(PEER_N_KEYS) + i_scr[1, 0:nb, :]
        off += nb
    cs_scr[N_PAIRS:PAIR_ROWS, :] = jnp.full((PAIR_ROWS - N_PAIRS, tt), NEG_INF, F32)
    ci_scr[N_PAIRS:PAIR_ROWS, :] = jnp.zeros((PAIR_ROWS - N_PAIRS, tt), F32)
    ci = ci_scr[...]

    def stage2(n, carry):
        cs = cs_scr[...]
        m, pos = _first_max(cs, rows_p)
        hit = rows_p == pos
        cs_scr[...] = jnp.where(hit, NEG_INF, cs)
        bs_scr[pl.ds(h * k + n, 1), :] = m
        be_scr[pl.ds(h * k + n, 1), :] = jnp.max(jnp.where(hit, ci, -1.0), axis=0, keepdims=True)
        return carry

    lax.fori_loop(0, k, stage2, 0)

    @pl.when(h == PEER_HEADS - 1)
    def _():
        gates = []
        for hh in range(PEER_HEADS):
            blk = bs_scr[hh * k:(hh + 1) * k, :]
            p = jnp.exp(blk - jnp.max(blk, axis=0, keepdims=True))
            gates.append(p / jnp.sum(p, axis=0, keepdims=True))
        gate_ref[...] = jnp.concatenate(gates, axis=0).T
        idx_ref[...] = (be_scr[...].T * float(WORD_ROWS)).astype(I32)


def _peer_topk(scores_t):
    n_chunks, n_keys, T = scores_t.shape
    tt = 512
    return pl.pallas_call(
        _peer_topk_kernel,
        grid=(T // tt, PEER_HEADS),
        in_specs=[pl.BlockSpec((2, n_keys, tt), lambda i, h: (h, 0, i))],
        out_specs=[pl.BlockSpec((tt, PEER_SEL), lambda i, h: (i, 0)),
                   pl.BlockSpec((tt, PEER_SEL), lambda i, h: (i, 0))],
        out_shape=[jax.ShapeDtypeStruct((T, PEER_SEL), I32),
                   jax.ShapeDtypeStruct((T, PEER_SEL), F32)],
        scratch_shapes=[pltpu.VMEM((2, n_keys, tt), F32),
                        pltpu.VMEM((2, PEER_TOPK, tt), F32), pltpu.VMEM((2, PEER_TOPK, tt), F32),
                        pltpu.VMEM((PAIR_ROWS, tt), F32), pltpu.VMEM((PAIR_ROWS, tt), F32),
                        pltpu.VMEM((PEER_SEL, tt), F32), pltpu.VMEM((PEER_SEL, tt), F32)],
        compiler_params=_params("arbitrary", "arbitrary"),
        name="peer_topk",
    )(scores_t)


def _gather_experts(idx_ref, t, tab_ref, stage_ref):
    for kk in range(PEER_SEL):
        r0 = pl.multiple_of(idx_ref[t, kk], WORD_ROWS)
        stage_ref[kk * WORD_ROWS:(kk + 1) * WORD_ROWS, :] = tab_ref[pl.ds(r0, WORD_ROWS), :]


def _staged(stage_ref):
    return pltpu.bitcast(stage_ref[...], BF16)


def _idx_copy(idx_hbm, idx_smem, sems, half_block, slot):
    start = pl.multiple_of(half_block * IDX_HALF, IDX_HALF)
    return pltpu.make_async_copy(idx_hbm.at[pl.ds(start, IDX_HALF)], idx_smem.at[slot], sems.at[slot])


def _for_each_token(idx_hbm, idx_smem, sems, tab_ref, stage_a, stage_b, compute):
    i = pl.program_id(0)
    n = pl.num_programs(0)
    stages = (stage_a, stage_b)

    @pl.when(i == 0)
    def _():
        _idx_copy(idx_hbm, idx_smem, sems, 0, 0).start()

    n_tok = 2 * IDX_HALF
    _idx_copy(idx_hbm, idx_smem, sems, 2 * i, 0).wait()
    _idx_copy(idx_hbm, idx_smem, sems, 2 * i + 1, 1).start()
    _gather_experts(idx_smem.at[0], 0, tab_ref, stages[0])
    for t in range(n_tok):
        if t + 1 == IDX_HALF:
            _idx_copy(idx_hbm, idx_smem, sems, 2 * i + 1, 1).wait()

            @pl.when(i + 1 < n)
            def _():
                _idx_copy(idx_hbm, idx_smem, sems, 2 * i + 2, 0).start()

        if t + 1 < n_tok:
            nxt = t + 1
            _gather_experts(idx_smem.at[nxt // IDX_HALF], nxt % IDX_HALF, tab_ref, stages[nxt % 2])
        compute(t, _staged(stages[t % 2]))


def _diag_mask():
    n = PEER_SEL * SUBLANES
    row = lax.broadcasted_iota(I32, (SUBLANES, n), 0)
    col = lax.broadcasted_iota(I32, (SUBLANES, n), 1)
    return (col % SUBLANES == row).astype(F32)


def _peer_down_kernel(idx_hbm, x_ref, gate_ref, tab_ref, sel_ref, coef_ref, stage_a, stage_b, r_ref,
                      idx_smem, sems):
    mask = _diag_mask()

    def compute(t, w):
        x_tile = x_ref[pl.ds(t, 1), :].reshape(SUBLANES, LANES)
        y = _dot_nt(x_tile.astype(BF16), w)
        r_ref[pl.ds(t, 1), :] = jnp.sum(y * mask, axis=0, keepdims=True)

    _for_each_token(idx_hbm, idx_smem, sems, tab_ref, stage_a, stage_b, compute)
    act = _dot_hilo(r_ref[...], sel_ref[...])
    gelu = 0.5 * act * (1.0 + lax.erf(act * (2.0 ** -0.5)))
    coef_ref[...] = gate_ref[...] * gelu


def _peer_down(idx, xn, gate, tab, sel):
    T = idx.shape[0]
    tb = 2 * IDX_HALF
    n = PEER_SEL * SUBLANES
    return pl.pallas_call(
        _peer_down_kernel,
        grid=(T // tb,),
        in_specs=[
            pl.BlockSpec(memory_space=pl.ANY),
            pl.BlockSpec((tb, n), lambda i: (i, 0)),
            pl.BlockSpec((tb, PEER_SEL), lambda i: (i, 0)),
            pl.BlockSpec(memory_space=pltpu.VMEM),
            pl.BlockSpec(memory_space=pltpu.VMEM),
        ],
        out_specs=pl.BlockSpec((tb, PEER_SEL), lambda i: (i, 0)),
        out_shape=jax.ShapeDtypeStruct((T, PEER_SEL), F32),
        scratch_shapes=[pltpu.VMEM((PEER_SEL * WORD_ROWS, LANES), jnp.uint32), pltpu.VMEM((PEER_SEL * WORD_ROWS, LANES), jnp.uint32),
                        pltpu.VMEM((tb, n), F32), pltpu.SMEM((2, IDX_HALF, PEER_SEL), I32), pltpu.SemaphoreType.DMA((2,))],
        compiler_params=_params("arbitrary"),
        name="peer_down",
    )(idx, xn, gate, tab, sel)


def _peer_up_kernel(idx_hbm, coef_ref, h_ref, tab_ref, rep_ref, o_ref, stage_a, stage_b, crep_ref,
                    idx_smem, sems):
    mask = _diag_mask()
    crep_ref[...] = _dot(coef_ref[...].astype(BF16), rep_ref[...])

    def compute(t, w):
        cm = (crep_ref[pl.ds(t, 1), :] * mask).astype(BF16)
        o_tile = h_ref[pl.ds(t, 1), :].reshape(SUBLANES, LANES) + _dot(cm, w)
        o_ref[pl.ds(t, 1), :] = o_tile.reshape(1, SUBLANES * LANES)

    _for_each_token(idx_hbm, idx_smem, sems, tab_ref, stage_a, stage_b, compute)


def _peer_up(idx, coef, h2, tab, rep):
    T = idx.shape[0]
    tb = 2 * IDX_HALF
    n = PEER_SEL * SUBLANES
    return pl.pallas_call(
        _peer_up_kernel,
        grid=(T // tb,),
        in_specs=[
            pl.BlockSpec(memory_space=pl.ANY),
            pl.BlockSpec((tb, PEER_SEL), lambda i: (i, 0)),
            pl.BlockSpec((tb, n), lambda i: (i, 0)),
            pl.BlockSpec(memory_space=pltpu.VMEM),
            pl.BlockSpec(memory_space=pltpu.VMEM),
        ],
        out_specs=pl.BlockSpec((tb, n), lambda i: (i, 0)),
        out_shape=jax.ShapeDtypeStruct((T, n), F32),
        scratch_shapes=[pltpu.VMEM((PEER_SEL * WORD_ROWS, LANES), jnp.uint32), pltpu.VMEM((PEER_SEL * WORD_ROWS, LANES), jnp.uint32),
                        pltpu.VMEM((tb, n), F32),
                        pltpu.SMEM((2, IDX_HALF, PEER_SEL), I32), pltpu.SemaphoreType.DMA((2,))],
        compiler_params=_params("arbitrary"),
        name="peer_up",
    )(idx, coef, h2, tab, rep)


def _pack_table(tab):
    e, d = tab.shape
    t = tab.astype(BF16).reshape(e, WORD_ROWS, 2, LANES)
    t = jnp.swapaxes(t, 2, 3)
    return lax.bitcast_convert_type(t, jnp.uint32).reshape(e * WORD_ROWS, LANES)


def _peer_layer(h2, gain, w_q, sub_keys, tab_u, tab_v):
    T, D = h2.shape
    n_rep = PEER_SEL * SUBLANES
    rep_lane = jnp.arange(n_rep)
    rep = (jnp.arange(PEER_SEL)[:, None] == rep_lane[None, :] // SUBLANES).astype(BF16)
    scores, xn = _peer_scores(h2, gain.reshape(1, D), w_q.astype(BF16), sub_keys.astype(BF16))
    idx, gate = _peer_topk(scores)
    coef = _peer_down(idx, xn, gate, _pack_table(tab_u), rep.T)
    return _peer_up(idx, coef, h2, _pack_table(tab_v), rep)


def _even_layer(h2, gain, w_in, pool_w, pool_scale, q_gain, k_gain, w_out, batch, seq_len):
    T, D = h2.shape
    aw = ATTN_HEADS * ATTN_HEAD_DIM
    lane = jnp.arange(aw)
    bd = ((lane[:, None] // ATTN_HEAD_DIM == lane[None, :] // ATTN_HEAD_DIM)
          .astype(F32) / ATTN_HEAD_DIM).astype(BF16)
    qk_gain = jnp.stack([jnp.tile(q_gain, ATTN_HEADS), jnp.tile(k_gain, ATTN_HEADS)])
    proj = _even_in(h2, gain, w_in.astype(BF16), pool_w.astype(BF16),
                    pool_scale.reshape(1, -1), qk_gain, bd, seq_len)
    proj3 = proj.reshape(batch, seq_len, -1)
    outs, lses = [], []
    for _, dilation in DILATED_BRANCHES:
        o, lse = _attn_branch(proj3, dilation)
        outs.append(o.reshape(T, aw))
        lses.append(lse.reshape(T, aw))
    return _even_out(proj, outs, lses, w_out.astype(BF16), h2)


def _ssm_layer(h2, gain, w_in, conv_w, conv_b, dt_bias, a_log, d_skip, gate_gain, w_out,
               batch, seq_len):
    inner = w_out.shape[0]
    n_heads = inner // SSM_HEAD_DIM
    main_w = 2 * inner + 2 * SSM_GROUPS * SSM_STATE
    zxbc = _ssm_in(h2, gain, w_in[:, :main_w].astype(BF16), conv_w, conv_b.reshape(1, -1),
                   seq_len, inner)
    w_dt = jnp.pad(w_in[:, main_w:], ((0, 0), (0, LANES - n_heads))).astype(BF16)
    b_dt = jnp.pad(dt_bias, (0, LANES - n_heads)).reshape(1, LANES)
    dt = _dt_proj(h2, gain, w_dt, b_dt)
    a_row = jnp.pad(-jnp.exp(a_log), (0, LANES - n_heads)).reshape(1, LANES)
    dskip_x = jnp.repeat(d_skip, SSM_HEAD_DIM).reshape(1, inner)
    expand = (jnp.arange(LANES)[:, None] == jnp.arange(inner)[None, :] // SSM_HEAD_DIM).astype(BF16)
    y = _ssd(zxbc, dt, a_row, dskip_x, gate_gain.reshape(1, inner), expand, batch, inner)
    return _proj_res(y, w_out.astype(BF16), h2)


def kernel(x, attn_norm, ffn_norm, even_w_in, pool_w, pool_scale, q_gain, k_gain, even_w_out, ssm_w_in, conv_w, conv_b, dt_bias, a_log, d_skip, gate_gain, ssm_w_out, peer_w_q, peer_sub_keys, peer_u, peer_v):
    B, S, D = x.shape
    T = B * S
    assert D == SUBLANES * LANES and S % ROW_TILE == 0
    assert all(S % (d * ATTN_BLOCK) == 0 for _, d in DILATED_BRANCHES)
    assert all(w // d == ATTN_BLOCK for w, d in DILATED_BRANCHES)
    h = x.reshape(T, D)
    for layer in range(attn_norm.shape[0]):
        i = layer // 2
        gain = attn_norm[layer].reshape(1, D)
        if layer % 2 == 0:
            h = _even_layer(h, gain, even_w_in[i], pool_w[i], pool_scale[i], q_gain[i], k_gain[i],
                            even_w_out[i], B, S)
        else:
            h = _ssm_layer(h, gain, ssm_w_in[i], conv_w[i], conv_b[i], dt_bias[i], a_log[i],
                           d_skip[i], gate_gain[i], ssm_w_out[i], B, S)
        h = _peer_layer(h, ffn_norm[layer], peer_w_q[layer], peer_sub_keys[layer],
                        peer_u[layer], peer_v[layer])
    return h.reshape(B, S, D)
```

```python
import functools
import math

import jax
import jax.numpy as jnp
from jax import lax
from jax.experimental import pallas as pl
from jax.experimental.pallas import tpu as pltpu

F32 = jnp.float32
BF16 = jnp.bfloat16
I32 = jnp.int32
EPS = 1e-6
NEG_INF = float("-inf")

LANES = 128
SUBLANES = 8
VMEM_LIMIT = 56 * 1024 * 1024

POOL_WINDOWS = (2, 4, 8, 16)
POOL_HALO = 16
ATTN_HEADS = 8
ATTN_HEAD_DIM = 64
ATTN_BLOCK = 128
DILATED_BRANCHES = ((128, 1), (512, 4), (2048, 16))
SSM_HEAD_DIM = 64
SSM_GROUPS = 8
SSM_STATE = 128
SSM_CONV = 4
SSM_CHUNK = 128
CONV_HALO = 8
PEER_HEADS = 8
PEER_N_KEYS = 128
PEER_TOPK = 16
PEER_SEL = PEER_HEADS * PEER_TOPK
WORD_ROWS = 4
IDX_HALF = 32
ROW_TILE = 512
COL_TILE = 512


def _params(*sem):
    return pltpu.CompilerParams(dimension_semantics=sem, vmem_limit_bytes=VMEM_LIMIT)


def _dot(a, b):
    return jnp.dot(a, b, preferred_element_type=F32)


def _dot_nt(a, b):
    return lax.dot_general(a, b, (((1,), (1,)), ((), ())), preferred_element_type=F32)


def _dot_hilo(a, b_bf16):
    hi = a.astype(BF16)
    lo = (a - hi.astype(F32)).astype(BF16)
    return _dot(hi, b_bf16) + _dot(lo, b_bf16)


def _rms_rows(x, gain):
    ms = jnp.mean(x * x, axis=-1, keepdims=True)
    return x * lax.rsqrt(ms + EPS) * gain


def _sigmoid(x):
    return 1.0 / (1.0 + jnp.exp(-x))


def _window_sum(ext, w, rows):
    e = ext
    span = 1
    while span < w:
        n = e.shape[0]
        e = e[span:n, :] + e[0:n - span, :]
        span *= 2
    start = POOL_HALO + 1 - w
    return e[start:start + rows, :]


def _even_in_kernel(seq_len, h_ref, g_ref, w_ref, pw_ref, ps_ref, qkg_ref, bd_ref, o_ref,
                    xn_ref, carry_ref):
    i = pl.program_id(0)
    j = pl.program_id(1)
    tm = h_ref.shape[0]

    @pl.when(j == 0)
    def _():
        xn_ref[...] = _rms_rows(h_ref[...], g_ref[...]).astype(BF16)

    acc = _dot(xn_ref[...], w_ref[...])

    @pl.when(j == 0)
    def _():
        pos0 = (i * tm) % seq_len

        @pl.when(pos0 == 0)
        def _():
            carry_ref[...] = jnp.zeros_like(carry_ref)

        ext = jnp.concatenate([carry_ref[...], acc], axis=0)
        pos = pos0 + lax.broadcasted_iota(I32, (tm, 1), 0)
        outs = []
        for gi, w in enumerate(POOL_WINDOWS):
            sl = slice(gi * LANES, (gi + 1) * LANES)
            ws = _window_sum(ext[:, sl], w, tm)
            cnt = jnp.minimum(pos + 1, w).astype(F32)
            mixed = ws / cnt - acc[:, sl]
            outs.append(_dot(mixed.astype(BF16), pw_ref[gi]))
        o_ref[...] = jnp.concatenate(outs, axis=1) * ps_ref[...]
        carry_ref[...] = acc[tm - POOL_HALO:tm, :]

    def qk_norm(row):
        ms = _dot_hilo(acc * acc, bd_ref[...])
        o_ref[...] = acc * lax.rsqrt(ms + EPS) * qkg_ref[row:row + 1, :]

    @pl.when(j == 1)
    def _():
        qk_norm(0)

    @pl.when(j == 2)
    def _():
        qk_norm(1)

    @pl.when(j == 3)
    def _():
        o_ref[...] = acc


def _even_in(h2, gain, w_in, pool_w, pool_scale, qk_gain, bd, seq_len):
    T, D = h2.shape
    N = w_in.shape[1]
    tm, tn = ROW_TILE, COL_TILE
    return pl.pallas_call(
        functools.partial(_even_in_kernel, seq_len),
        grid=(T // tm, N // tn),
        in_specs=[
            pl.BlockSpec((tm, D), lambda i, j: (i, 0)),
            pl.BlockSpec((1, D), lambda i, j: (0, 0)),
            pl.BlockSpec((D, tn), lambda i, j: (0, j)),
            pl.BlockSpec(pool_w.shape, lambda i, j: (0, 0, 0)),
            pl.BlockSpec((1, tn), lambda i, j: (0, 0)),
            pl.BlockSpec((2, tn), lambda i, j: (0, 0)),
            pl.BlockSpec((tn, tn), lambda i, j: (0, 0)),
        ],
        out_specs=pl.BlockSpec((tm, tn), lambda i, j: (i, j)),
        out_shape=jax.ShapeDtypeStruct((T, N), F32),
        scratch_shapes=[pltpu.VMEM((tm, D), BF16), pltpu.VMEM((POOL_HALO, tn), F32)],
        compiler_params=_params("arbitrary", "arbitrary"),
        name="even_in",
    )(h2, gain, w_in, pool_w, pool_scale, qk_gain, bd)


def _attn_kernel(dilation, q_ref, kp_ref, kc_ref, vp_ref, vc_ref, o_ref, lse_ref):
    n = pl.program_id(2)
    qb = ATTN_BLOCK
    qi = lax.broadcasted_iota(I32, (qb, qb), 0)
    kj = lax.broadcasted_iota(I32, (qb, qb), 1)
    lane = lax.broadcasted_iota(I32, (qb, LANES), 1)
    valid_c = kj <= qi
    valid_p = (kj >= qi) & (n > 0)
    dist_c = ((qi - kj) * dilation).astype(F32)
    dist_p = ((qi + qb - kj) * dilation).astype(F32)
    scale = ATTN_HEAD_DIM ** -0.5
    heads_per_tile = LANES // ATTN_HEAD_DIM
    for hp in range(ATTN_HEADS // heads_per_tile):
        sl = slice(hp * LANES, (hp + 1) * LANES)
        qp = q_ref[:, sl]
        kc = kc_ref[:, sl].astype(BF16)
        kp = kp_ref[:, sl].astype(BF16)
        vc = vc_ref[:, sl].astype(BF16)
        vp = vp_ref[:, sl].astype(BF16)
        o_tile = jnp.zeros((qb, LANES), F32)
        lse_tile = jnp.zeros((qb, LANES), F32)
        for e in range(heads_per_tile):
            head = hp * heads_per_tile + e
            slope = 2.0 ** (-8.0 * (head + 1) / ATTN_HEADS)
            in_head = (lane >= e * ATTN_HEAD_DIM) & (lane < (e + 1) * ATTN_HEAD_DIM)
            qm = jnp.where(in_head, qp, 0.0).astype(BF16)
            s_c = _dot_nt(qm, kc) * scale - slope * dist_c
            s_p = _dot_nt(qm, kp) * scale - slope * dist_p
            s_c = jnp.where(valid_c, s_c, NEG_INF)
            s_p = jnp.where(valid_p, s_p, NEG_INF)
            m = jnp.maximum(jnp.max(s_c, axis=-1, keepdims=True),
                            jnp.max(s_p, axis=-1, keepdims=True))
            p_c = jnp.exp(s_c - m)
            p_p = jnp.exp(s_p - m)
            l = jnp.sum(p_c, axis=-1, keepdims=True) + jnp.sum(p_p, axis=-1, keepdims=True)
            o = (_dot(p_c.astype(BF16), vc) + _dot(p_p.astype(BF16), vp)) / l
            lse = m + jnp.log(l)
            o_tile = jnp.where(in_head, o, o_tile)
            lse_tile = jnp.where(in_head, lse, lse_tile)
        o_ref[:, sl] = o_tile
        lse_ref[:, sl] = lse_tile


def _attn_branch(proj, dilation):
    B, S, W = proj.shape
    aw = ATTN_HEADS * ATTN_HEAD_DIM
    L = S // dilation
    nb = L // ATTN_BLOCK
    cols = W // aw
    pv = proj.reshape(B, L, dilation * W)
    blk = (None, ATTN_BLOCK, aw)

    def spec(col, prev):
        if prev:
            return pl.BlockSpec(blk, lambda b, r, n: (b, jnp.maximum(n - 1, 0), r * cols + col))
        return pl.BlockSpec(blk, lambda b, r, n: (b, n, r * cols + col))

    out_spec = pl.BlockSpec(blk, lambda b, r, n: (b, n, r))
    o, lse = pl.pallas_call(
        functools.partial(_attn_kernel, dilation),
        grid=(B, dilation, nb),
        in_specs=[spec(1, False), spec(2, True), spec(2, False), spec(3, True), spec(3, False)],
        out_specs=[out_spec, out_spec],
        out_shape=[jax.ShapeDtypeStruct((B, L, dilation * aw), F32)] * 2,
        compiler_params=_params("arbitrary", "arbitrary", "arbitrary"),
        name="attn",
    )(pv, pv, pv, pv, pv)
    return o.reshape(B, S, aw), lse.reshape(B, S, aw)


def _even_out_kernel(a_ref, o1, o2, o3, l1, l2, l3, wa_ref, wo_ref, h_ref, out_ref, mix_ref):
    j = pl.program_id(1)
    half = a_ref.shape[1]

    @pl.when(j == 0)
    def _():
        la, lb, lc = l1[...], l2[...], l3[...]
        mx = jnp.maximum(jnp.maximum(la, lb), lc)
        wa, wb, wc = jnp.exp(la - mx), jnp.exp(lb - mx), jnp.exp(lc - mx)
        den = wa + wb + wc
        o = (wa / den) * o1[...] + (wb / den) * o2[...] + (wc / den) * o3[...]
        mix_ref[:, 0:half] = a_ref[...].astype(BF16)
        mix_ref[:, half:2 * half] = o.astype(BF16)

    out_ref[...] = (h_ref[...] + _dot(mix_ref[:, 0:half], wa_ref[...])
                    + _dot(mix_ref[:, half:2 * half], wo_ref[...]))


def _even_out(proj2, os_, lses, w_out, h2):
    T, D = h2.shape
    half = w_out.shape[0] // 2
    tm, tn = ROW_TILE, COL_TILE
    row = pl.BlockSpec((tm, half), lambda i, j: (i, 0))
    return pl.pallas_call(
        _even_out_kernel,
        grid=(T // tm, D // tn),
        in_specs=[row] * 7 + [
            pl.BlockSpec((half, tn), lambda i, j: (0, j)),
            pl.BlockSpec((half, tn), lambda i, j: (1, j)),
            pl.BlockSpec((tm, tn), lambda i, j: (i, j)),
        ],
        out_specs=pl.BlockSpec((tm, tn), lambda i, j: (i, j)),
        out_shape=jax.ShapeDtypeStruct((T, D), F32),
        scratch_shapes=[pltpu.VMEM((tm, 2 * half), BF16)],
        compiler_params=_params("arbitrary", "arbitrary"),
        name="even_out",
    )(proj2, *os_, *lses, w_out, w_out, h2)


def _ssm_in_kernel(seq_len, n_plain, h_ref, g_ref, w_ref, cw_ref, cb_ref, o_ref, xn_ref, carry_ref):
    i = pl.program_id(0)
    j = pl.program_id(1)
    tm = h_ref.shape[0]

    @pl.when(j == 0)
    def _():
        xn_ref[...] = _rms_rows(h_ref[...], g_ref[...]).astype(BF16)

    acc = _dot(xn_ref[...], w_ref[...])

    @pl.when(j < n_plain)
    def _():
        o_ref[...] = acc

    @pl.when(j >= n_plain)
    def _():
        jc = j - n_plain
        pos0 = (i * tm) % seq_len

        @pl.when(pos0 == 0)
        def _():
            carry_ref[jc] = jnp.zeros(carry_ref.shape[1:], F32)

        ext = jnp.concatenate([carry_ref[jc], acc], axis=0)
        y = cb_ref[...]
        for k in range(SSM_CONV):
            off = CONV_HALO - (SSM_CONV - 1) + k
            y = y + cw_ref[k:k + 1, :] * ext[off:off + tm, :]
        o_ref[...] = y * _sigmoid(y)
        carry_ref[jc] = acc[tm - CONV_HALO:tm, :]


def _ssm_in(h2, gain, w_main, conv_w, conv_b, seq_len, inner):
    T, D = h2.shape
    N = w_main.shape[1]
    tm, tn = ROW_TILE, COL_TILE
    n_plain = inner // tn
    n_conv = N // tn - n_plain
    return pl.pallas_call(
        functools.partial(_ssm_in_kernel, seq_len, n_plain),
        grid=(T // tm, N // tn),
        in_specs=[
            pl.BlockSpec((tm, D), lambda i, j: (i, 0)),
            pl.BlockSpec((1, D), lambda i, j: (0, 0)),
            pl.BlockSpec((D, tn), lambda i, j: (0, j)),
            pl.BlockSpec((SSM_CONV, tn), lambda i, j: (0, jnp.maximum(j - n_plain, 0))),
            pl.BlockSpec((1, tn), lambda i, j: (0, jnp.maximum(j - n_plain, 0))),
        ],
        out_specs=pl.BlockSpec((tm, tn), lambda i, j: (i, j)),
        out_shape=jax.ShapeDtypeStruct((T, N), F32),
        scratch_shapes=[pltpu.VMEM((tm, D), BF16), pltpu.VMEM((n_conv, CONV_HALO, tn), F32)],
        compiler_params=_params("arbitrary", "arbitrary"),
        name="ssm_in",
    )(h2, gain, w_main, conv_w, conv_b)


def _dt_kernel(h_ref, g_ref, w_ref, b_ref, o_ref):
    xn = _rms_rows(h_ref[...], g_ref[...]).astype(BF16)
    raw = _dot(xn, w_ref[...]) + b_ref[...]
    o_ref[...] = jnp.maximum(raw, 0.0) + jnp.log(1.0 + jnp.exp(-jnp.abs(raw)))


def _dt_proj(h2, gain, w_dt, dt_bias):
    T, D = h2.shape
    tm = ROW_TILE
    return pl.pallas_call(
        _dt_kernel,
        grid=(T // tm,),
        in_specs=[
            pl.BlockSpec((tm, D), lambda i: (i, 0)),
            pl.BlockSpec((1, D), lambda i: (0, 0)),
            pl.BlockSpec((D, LANES), lambda i: (0, 0)),
            pl.BlockSpec((1, LANES), lambda i: (0, 0)),
        ],
        out_specs=pl.BlockSpec((tm, LANES), lambda i: (i, 0)),
        out_shape=jax.ShapeDtypeStruct((T, LANES), F32),
        compiler_params=_params("arbitrary"),
        name="dt",
    )(h2, gain, w_dt, dt_bias)


def _ssd_kernel(z_ref, x_ref, b_ref, c_ref, dt_ref, a_ref, dsk_ref, gg_ref, ex_ref, o_ref, st_ref):
    c = pl.program_id(1)
    q = SSM_CHUNK
    gw = x_ref.shape[1] // SSM_GROUPS
    hpg = gw // SSM_HEAD_DIM

    @pl.when(c == 0)
    def _():
        st_ref[...] = jnp.zeros_like(st_ref)

    li = lax.broadcasted_iota(I32, (q, q), 0)
    si = lax.broadcasted_iota(I32, (q, q), 1)
    causal = si <= li
    tril = jnp.where(causal, 1.0, 0.0).astype(BF16)

    dt = dt_ref[...]
    da = dt * a_ref[...]
    acum = _dot_hilo_rhs(tril, da)
    acum_t = acum.T
    ex = ex_ref[...]
    dt_x = _dot_hilo(dt, ex)
    acum_x = _dot_hilo(acum, ex)
    last_x = acum_x[q - 1:q, :]
    x = x_ref[...]
    xdt = x * dt_x
    grow = jnp.exp(acum_x)
    xw = (xdt * jnp.exp(last_x - acum_x)).astype(BF16)
    sdec = jnp.exp(last_x)
    xdt_b = xdt.astype(BF16)

    for g in range(SSM_GROUPS):
        gs = slice(g * gw, (g + 1) * gw)
        bg = b_ref[:, g * SSM_STATE:(g + 1) * SSM_STATE]
        cg = c_ref[:, g * SSM_STATE:(g + 1) * SSM_STATE].astype(BF16)
        cb = _dot_nt(cg, bg.astype(BF16))
        bg_t = bg.T.astype(BF16)
        yd = []
        for r in range(hpg):
            hh = g * hpg + r
            seg = acum[:, hh:hh + 1] - acum_t[hh:hh + 1, :]
            decay = jnp.exp(jnp.where(causal, seg, NEG_INF))
            m = (cb * decay).astype(BF16)
            yd.append(_dot(m, xdt_b[:, g * gw + r * SSM_HEAD_DIM:g * gw + (r + 1) * SSM_HEAD_DIM]))
        y = jnp.concatenate(yd, axis=1)
        st = st_ref[g]
        y = y + _dot(cg, st.astype(BF16)) * grow[:, gs]
        st_ref[g] = st * sdec[:, gs] + _dot(bg_t, xw[:, gs])
        y = y + dsk_ref[:, gs] * x[:, gs]
        zg = z_ref[:, gs]
        y = y * (zg * _sigmoid(zg))
        o_ref[:, gs] = _rms_rows(y, gg_ref[:, gs])


def _dot_hilo_rhs(a_bf16, b):
    hi = b.astype(BF16)
    lo = (b - hi.astype(F32)).astype(BF16)
    return _dot(a_bf16, hi) + _dot(a_bf16, lo)


def _ssd(zxbc, dt, a_row, dskip_x, gate_gain, expand, batch, inner):
    T = zxbc.shape[0]
    q = SSM_CHUNK
    nc = T // batch // q
    bw = SSM_GROUPS * SSM_STATE
    row = lambda col: (lambda b, c: (b * nc + c, col))
    full = lambda b, c: (0, 0)
    return pl.pallas_call(
        _ssd_kernel,
        grid=(batch, nc),
        in_specs=[
            pl.BlockSpec((q, inner), row(0)),
            pl.BlockSpec((q, inner), row(1)),
            pl.BlockSpec((q, bw), row(2 * inner // bw)),
            pl.BlockSpec((q, bw), row(2 * inner // bw + 1)),
            pl.BlockSpec((q, LANES), row(0)),
            pl.BlockSpec((1, LANES), full),
            pl.BlockSpec((1, inner), full),
            pl.BlockSpec((1, inner), full),
            pl.BlockSpec((LANES, inner), full),
        ],
        out_specs=pl.BlockSpec((q, inner), row(0)),
        out_shape=jax.ShapeDtypeStruct((T, inner), F32),
        scratch_shapes=[pltpu.VMEM((SSM_GROUPS, SSM_STATE, inner // SSM_GROUPS), F32)],
        compiler_params=_params("arbitrary", "arbitrary"),
        name="ssd",
    )(zxbc, zxbc, zxbc, zxbc, dt, a_row, dskip_x, gate_gain, expand)


def _proj_res_kernel(x_ref, w_ref, h_ref, o_ref):
    o_ref[...] = h_ref[...] + _dot(x_ref[...].astype(BF16), w_ref[...])


def _proj_res(x2, w, h2):
    T, K = x2.shape
    D = w.shape[1]
    tm, tn = ROW_TILE, COL_TILE
    return pl.pallas_call(
        _proj_res_kernel,
        grid=(T // tm, D // tn),
        in_specs=[
            pl.BlockSpec((tm, K), lambda i, j: (i, 0)),
            pl.BlockSpec((K, tn), lambda i, j: (0, j)),
            pl.BlockSpec((tm, tn), lambda i, j: (i, j)),
        ],
        out_specs=pl.BlockSpec((tm, tn), lambda i, j: (i, j)),
        out_shape=jax.ShapeDtypeStruct((T, D), F32),
        compiler_params=_params("arbitrary", "arbitrary"),
        name="proj_res",
    )(x2, w, h2)


def _peer_score_kernel(h_ref, g_ref, w_ref, k_ref, s_ref, xn_out_ref, xn_ref):
    j = pl.program_id(1)

    @pl.when(j == 0)
    def _():
        xn = _rms_rows(h_ref[...], g_ref[...])
        xn_out_ref[...] = xn
        xn_ref[...] = xn.astype(BF16)

    qv = _dot(xn_ref[...], w_ref[...])
    for cidx in range(qv.shape[1] // LANES):
        qc = qv[:, cidx * LANES:(cidx + 1) * LANES]
        hi = qc.astype(BF16)
        lo = (qc - hi.astype(F32)).astype(BF16)
        keys = k_ref[cidx % 2]
        s_ref[cidx] = _dot_nt(keys, hi) + _dot_nt(keys, lo)


def _peer_scores(h2, gain, w_q, keys):
    T, D = h2.shape
    N = w_q.shape[1]
    tm, tn = ROW_TILE, COL_TILE
    per = tn // LANES
    return pl.pallas_call(
        _peer_score_kernel,
        grid=(T // tm, N // tn),
        in_specs=[
            pl.BlockSpec((tm, D), lambda i, j: (i, 0)),
            pl.BlockSpec((1, D), lambda i, j: (0, 0)),
            pl.BlockSpec((D, tn), lambda i, j: (0, j)),
            pl.BlockSpec(keys.shape, lambda i, j: (0, 0, 0)),
        ],
        out_specs=[pl.BlockSpec((per, PEER_N_KEYS, tm), lambda i, j: (j, 0, i)),
                   pl.BlockSpec((tm, D), lambda i, j: (i, 0))],
        out_shape=[jax.ShapeDtypeStruct((N // LANES, PEER_N_KEYS, T), F32),
                   jax.ShapeDtypeStruct((T, D), F32)],
        scratch_shapes=[pltpu.VMEM((tm, D), BF16)],
        compiler_params=_params("arbitrary", "arbitrary"),
        name="peer_scores",
    )(h2, gain, w_q, keys)


PAIR_COUNTS = tuple(PEER_TOPK // (a + 1) for a in range(PEER_TOPK))
N_PAIRS = sum(PAIR_COUNTS)
PAIR_ROWS = -(-N_PAIRS // SUBLANES) * SUBLANES


def _first_max(v, rows):
    m = jnp.max(v, axis=0, keepdims=True)
    idx = jnp.min(jnp.where(v == m, rows, float(v.shape[0])), axis=0, keepdims=True)
    return m, idx


def _peer_topk_kernel(s_ref, idx_ref, gate_ref, s_scr, m_scr, i_scr, cs_scr, ci_scr,
                      bs_scr, be_scr):
    h = pl.program_id(1)
    tt = s_ref.shape[2]
    k = PEER_TOPK
    rows_k = lax.broadcasted_iota(I32, (PEER_N_KEYS, tt), 0).astype(F32)
    rows_p = lax.broadcasted_iota(I32, (PAIR_ROWS, tt), 0).astype(F32)

    s_scr[...] = s_ref[...]

    def stage1(a, carry):
        for half in range(2):
            v = s_scr[half]
            m, idx = _first_max(v, rows_k)
            s_scr[half] = jnp.where(rows_k == idx, NEG_INF, v)
            m_scr[half, pl.ds(a, 1), :] = m
            i_scr[half, pl.ds(a, 1), :] = idx
        return carry

    lax.fori_loop(0, k, stage1, 0)

    off = 0
    for a, nb in enumerate(PAIR_COUNTS):
        cs_scr[off:off + nb, :] = m_scr[0, a:a + 1, :] + m_scr[1, 0:nb, :]
        ci_scr[off:off + nb, :] = i_scr[0, a:a + 1, :] * float(PEER_N_KEYS) + i_scr[1, 0:nb, :]
        off += nb
    cs_scr[N_PAIRS:PAIR_ROWS, :] = jnp.full((PAIR_ROWS - N_PAIRS, tt), NEG_INF, F32)
    ci_scr[N_PAIRS:PAIR_ROWS, :] = jnp.zeros((PAIR_ROWS - N_PAIRS, tt), F32)
    ci = ci_scr[...]

    def stage2(n, carry):
        cs = cs_scr[...]
        m, pos = _first_max(cs, rows_p)
        hit = rows_p == pos
        cs_scr[...] = jnp.where(hit, NEG_INF, cs)
        bs_scr[pl.ds(h * k + n, 1), :] = m
        be_scr[pl.ds(h * k + n, 1), :] = jnp.max(jnp.where(hit, ci, -1.0), axis=0, keepdims=True)
        return carry

    lax.fori_loop(0, k, stage2, 0)

    @pl.when(h == PEER_HEADS - 1)
    def _():
        gates = []
        for hh in range(PEER_HEADS):
            blk = bs_scr[hh * k:(hh + 1) * k, :]
            p = jnp.exp(blk - jnp.max(blk, axis=0, keepdims=True))
            gates.append(p / jnp.sum(p, axis=0, keepdims=True))
        gate_ref[...] = jnp.concatenate(gates, axis=0).T
        idx_ref[...] = (be_scr[...].T * float(WORD_ROWS)).astype(I32)


def _peer_topk(scores_t):
    n_chunks, n_keys, T = scores_t.shape
    tt = 1024
    return pl.pallas_call(
        _peer_topk_kernel,
        grid=(T // tt, PEER_HEADS),
        in_specs=[pl.BlockSpec((2, n_keys, tt), lambda i, h: (h, 0, i))],
        out_specs=[pl.BlockSpec((tt, PEER_SEL), lambda i, h: (i, 0)),
                   pl.BlockSpec((tt, PEER_SEL), lambda i, h: (i, 0))],
        out_shape=[jax.ShapeDtypeStruct((T, PEER_SEL), I32),
                   jax.ShapeDtypeStruct((T, PEER_SEL), F32)],
        scratch_shapes=[pltpu.VMEM((2, n_keys, tt), F32),
                        pltpu.VMEM((2, PEER_TOPK, tt), F32), pltpu.VMEM((2, PEER_TOPK, tt), F32),
                        pltpu.VMEM((PAIR_ROWS, tt), F32), pltpu.VMEM((PAIR_ROWS, tt), F32),
                        pltpu.VMEM((PEER_SEL, tt), F32), pltpu.VMEM((PEER_SEL, tt), F32)],
        compiler_params=_params("arbitrary", "arbitrary"),
        name="peer_topk",
    )(scores_t)


def _gather_experts(idx_ref, t, tab_ref, stage_ref):
    for kk in range(PEER_SEL):
        r0 = pl.multiple_of(idx_ref[t, kk], WORD_ROWS)
        stage_ref[kk * WORD_ROWS:(kk + 1) * WORD_ROWS, :] = tab_ref[pl.ds(r0, WORD_ROWS), :]


def _staged(stage_ref):
    return pltpu.bitcast(stage_ref[...], BF16)


def _idx_copy(idx_hbm, idx_smem, sems, half_block, slot):
    start = pl.multiple_of(half_block * IDX_HALF, IDX_HALF)
    return pltpu.make_async_copy(idx_hbm.at[pl.ds(start, IDX_HALF)], idx_smem.at[slot], sems.at[slot])


def _for_each_token(idx_hbm, idx_smem, sems, tab_ref, stage_a, stage_b, compute):
    i = pl.program_id(0)
    n = pl.num_programs(0)
    stages = (stage_a, stage_b)

    @pl.when(i == 0)
    def _():
        _idx_copy(idx_hbm, idx_smem, sems, 0, 0).start()

    n_tok = 2 * IDX_HALF
    _idx_copy(idx_hbm, idx_smem, sems, 2 * i, 0).wait()
    _idx_copy(idx_hbm, idx_smem, sems, 2 * i + 1, 1).start()
    _gather_experts(idx_smem.at[0], 0, tab_ref, stages[0])
    for t in range(n_tok):
        if t + 1 == IDX_HALF:
            _idx_copy(idx_hbm, idx_smem, sems, 2 * i + 1, 1).wait()

            @pl.when(i + 1 < n)
            def _():
                _idx_copy(idx_hbm, idx_smem, sems, 2 * i + 2, 0).start()

        if t + 1 < n_tok:
            nxt = t + 1
            _gather_experts(idx_smem.at[nxt // IDX_HALF], nxt % IDX_HALF, tab_ref, stages[nxt % 2])
        compute(t, _staged(stages[t % 2]))


def _diag_mask():
    n = PEER_SEL * SUBLANES
    row = lax.broadcasted_iota(I32, (SUBLANES, n), 0)
    col = lax.broadcasted_iota(I32, (SUBLANES, n), 1)
    return (col % SUBLANES == row).astype(F32)


def _peer_down_kernel(idx_hbm, x_ref, gate_ref, tab_ref, sel_ref, coef_ref, stage_a, stage_b, r_ref,
                      idx_smem, sems):
    mask = _diag_mask()

    def compute(t, w):
        x_tile = x_ref[pl.ds(t, 1), :].reshape(SUBLANES, LANES)
        y = _dot_nt(x_tile.astype(BF16), w)
        r_ref[pl.ds(t, 1), :] = jnp.sum(y * mask, axis=0, keepdims=True)

    _for_each_token(idx_hbm, idx_smem, sems, tab_ref, stage_a, stage_b, compute)
    act = _dot_hilo(r_ref[...], sel_ref[...])
    gelu = 0.5 * act * (1.0 + lax.erf(act * (2.0 ** -0.5)))
    coef_ref[...] = gate_ref[...] * gelu


def _peer_down(idx, xn, gate, tab, sel):
    T = idx.shape[0]
    tb = 2 * IDX_HALF
    n = PEER_SEL * SUBLANES
    return pl.pallas_call(
        _peer_down_kernel,
        grid=(T // tb,),
        in_specs=[
            pl.BlockSpec(memory_space=pl.ANY),
            pl.BlockSpec((tb, n), lambda i: (i, 0)),
            pl.BlockSpec((tb, PEER_SEL), lambda i: (i, 0)),
            pl.BlockSpec(memory_space=pltpu.VMEM),
            pl.BlockSpec(memory_space=pltpu.VMEM),
        ],
        out_specs=pl.BlockSpec((tb, PEER_SEL), lambda i: (i, 0)),
        out_shape=jax.ShapeDtypeStruct((T, PEER_SEL), F32),
        scratch_shapes=[pltpu.VMEM((PEER_SEL * WORD_ROWS, LANES), jnp.uint32), pltpu.VMEM((PEER_SEL * WORD_ROWS, LANES), jnp.uint32),
                        pltpu.VMEM((tb, n), F32), pltpu.SMEM((2, IDX_HALF, PEER_SEL), I32), pltpu.SemaphoreType.DMA((2,))],
        compiler_params=_params("arbitrary"),
        name="peer_down",
    )(idx, xn, gate, tab, sel)


def _peer_up_kernel(idx_hbm, coef_ref, h_ref, tab_ref, rep_ref, o_ref, stage_a, stage_b, crep_ref,
                    idx_smem, sems):
    mask = _diag_mask()
    crep_ref[...] = _dot(coef_ref[...].astype(BF16), rep_ref[...])

    def compute(t, w):
        cm = (crep_ref[pl.ds(t, 1), :] * mask).astype(BF16)
        o_tile = h_ref[pl.ds(t, 1), :].reshape(SUBLANES, LANES) + _dot(cm, w)
        o_ref[pl.ds(t, 1), :] = o_tile.reshape(1, SUBLANES * LANES)

    _for_each_token(idx_hbm, idx_smem, sems, tab_ref, stage_a, stage_b, compute)


def _peer_up(idx, coef, h2, tab, rep):
    T = idx.shape[0]
    tb = 2 * IDX_HALF
    n = PEER_SEL * SUBLANES
    return pl.pallas_call(
        _peer_up_kernel,
        grid=(T // tb,),
        in_specs=[
            pl.BlockSpec(memory_space=pl.ANY),
            pl.BlockSpec((tb, PEER_SEL), lambda i: (i, 0)),
            pl.BlockSpec((tb, n), lambda i: (i, 0)),
            pl.BlockSpec(memory_space=pltpu.VMEM),
            pl.BlockSpec(memory_space=pltpu.VMEM),
        ],
        out_specs=pl.BlockSpec((tb, n), lambda i: (i, 0)),
        out_shape=jax.ShapeDtypeStruct((T, n), F32),
        scratch_shapes=[pltpu.VMEM((PEER_SEL * WORD_ROWS, LANES), jnp.uint32), pltpu.VMEM((PEER_SEL * WORD_ROWS, LANES), jnp.uint32),
                        pltpu.VMEM((tb, n), F32),
                        pltpu.SMEM((2, IDX_HALF, PEER_SEL), I32), pltpu.SemaphoreType.DMA((2,))],
        compiler_params=_params("arbitrary"),
        name="peer_up",
    )(idx, coef, h2, tab, rep)


def _pack_table(tab):
    e, d = tab.shape
    t = tab.astype(BF16).reshape(e, WORD_ROWS, 2, LANES)
    t = jnp.swapaxes(t, 2, 3)
    return lax.bitcast_convert_type(t, jnp.uint32).reshape(e * WORD_ROWS, LANES)


def _peer_layer(h2, gain, w_q, sub_keys, tab_u, tab_v):
    T, D = h2.shape
    n_rep = PEER_SEL * SUBLANES
    rep_lane = jnp.arange(n_rep)
    rep = (jnp.arange(PEER_SEL)[:, None] == rep_lane[None, :] // SUBLANES).astype(BF16)
    scores, xn = _peer_scores(h2, gain.reshape(1, D), w_q.astype(BF16), sub_keys.astype(BF16))
    idx, gate = _peer_topk(scores)
    coef = _peer_down(idx, xn, gate, _pack_table(tab_u), rep.T)
    return _peer_up(idx, coef, h2, _pack_table(tab_v), rep)


def _even_layer(h2, gain, w_in, pool_w, pool_scale, q_gain, k_gain, w_out, batch, seq_len):
    T, D = h2.shape
    aw = ATTN_HEADS * ATTN_HEAD_DIM
    lane = jnp.arange(aw)
    bd = ((lane[:, None] // ATTN_HEAD_DIM == lane[None, :] // ATTN_HEAD_DIM)
          .astype(F32) / ATTN_HEAD_DIM).astype(BF16)
    qk_gain = jnp.stack([jnp.tile(q_gain, ATTN_HEADS), jnp.tile(k_gain, ATTN_HEADS)])
    proj = _even_in(h2, gain, w_in.astype(BF16), pool_w.astype(BF16),
                    pool_scale.reshape(1, -1), qk_gain, bd, seq_len)
    proj3 = proj.reshape(batch, seq_len, -1)
    outs, lses = [], []
    for _, dilation in DILATED_BRANCHES:
        o, lse = _attn_branch(proj3, dilation)
        outs.append(o.reshape(T, aw))
        lses.append(lse.reshape(T, aw))
    return _even_out(proj, outs, lses, w_out.astype(BF16), h2)


def _ssm_layer(h2, gain, w_in, conv_w, conv_b, dt_bias, a_log, d_skip, gate_gain, w_out,
               batch, seq_len):
    inner = w_out.shape[0]
    n_heads = inner // SSM_HEAD_DIM
    main_w = 2 * inner + 2 * SSM_GROUPS * SSM_STATE
    zxbc = _ssm_in(h2, gain, w_in[:, :main_w].astype(BF16), conv_w, conv_b.reshape(1, -1),
                   seq_len, inner)
    w_dt = jnp.pad(w_in[:, main_w:], ((0, 0), (0, LANES - n_heads))).astype(BF16)
    b_dt = jnp.pad(dt_bias, (0, LANES - n_heads)).reshape(1, LANES)
    dt = _dt_proj(h2, gain, w_dt, b_dt)
    a_row = jnp.pad(-jnp.exp(a_log), (0, LANES - n_heads)).reshape(1, LANES)
    dskip_x = jnp.repeat(d_skip, SSM_HEAD_DIM).reshape(1, inner)
    expand = (jnp.arange(LANES)[:, None] == jnp.arange(inner)[None, :] // SSM_HEAD_DIM).astype(BF16)
    y = _ssd(zxbc, dt, a_row, dskip_x, gate_gain.reshape(1, inner), expand, batch, inner)
    return _proj_res(y, w_out.astype(BF16), h2)


def kernel(x, attn_norm, ffn_norm, even_w_in, pool_w, pool_scale, q_gain, k_gain, even_w_out, ssm_w_in, conv_w, conv_b, dt_bias, a_log, d_skip, gate_gain, ssm_w_out, peer_w_q, peer_sub_keys, peer_u, peer_v):
    B, S, D = x.shape
    T = B * S
    assert D == SUBLANES * LANES and S % ROW_TILE == 0
    assert all(S % (d * ATTN_BLOCK) == 0 for _, d in DILATED_BRANCHES)
    assert all(w // d == ATTN_BLOCK for w, d in DILATED_BRANCHES)
    h = x.reshape(T, D)
    for layer in range(attn_norm.shape[0]):
        i = layer // 2
        gain = attn_norm[layer].reshape(1, D)
        if layer % 2 == 0:
            h = _even_layer(h, gain, even_w_in[i], pool_w[i], pool_scale[i], q_gain[i], k_gain[i],
                            even_w_out[i], B, S)
        else:
            h = _ssm_layer(h, gain, ssm_w_in[i], conv_w[i], conv_b[i], dt_bias[i], a_log[i],
                           d_skip[i], gate_gain[i], ssm_w_out[i], B, S)
        h = _peer_layer(h, ffn_norm[layer], peer_w_q[layer], peer_sub_keys[layer],
                        peer_u[layer], peer_v[layer])
    return h.reshape(B, S, D)
```

```python
import functools
import math

import jax
import jax.numpy as jnp
from jax import lax
from jax.experimental import pallas as pl
from jax.experimental.pallas import tpu as pltpu

F32 = jnp.float32
BF16 = jnp.bfloat16
I32 = jnp.int32
EPS = 1e-6
NEG_INF = float("-inf")

LANES = 128
SUBLANES = 8
VMEM_LIMIT = 56 * 1024 * 1024

POOL_WINDOWS = (2, 4, 8, 16)
POOL_HALO = 16
ATTN_HEADS = 8
ATTN_HEAD_DIM = 64
ATTN_BLOCK = 128
DILATED_BRANCHES = ((128, 1), (512, 4), (2048, 16))
SSM_HEAD_DIM = 64
SSM_GROUPS = 8
SSM_STATE = 128
SSM_CONV = 4
SSM_CHUNK = 128
CONV_HALO = 8
PEER_HEADS = 8
PEER_N_KEYS = 128
PEER_TOPK = 16
PEER_SEL = PEER_HEADS * PEER_TOPK
WORD_ROWS = 4
IDX_HALF = 32
ROW_TILE = 1024
COL_TILE = 512


def _params(*sem):
    return pltpu.CompilerParams(dimension_semantics=sem, vmem_limit_bytes=VMEM_LIMIT)


def _dot(a, b):
    return jnp.dot(a, b, preferred_element_type=F32)


def _dot_nt(a, b):
    return lax.dot_general(a, b, (((1,), (1,)), ((), ())), preferred_element_type=F32)


def _dot_hilo(a, b_bf16):
    hi = a.astype(BF16)
    lo = (a - hi.astype(F32)).astype(BF16)
    return _dot(hi, b_bf16) + _dot(lo, b_bf16)


def _rms_rows(x, gain):
    ms = jnp.mean(x * x, axis=-1, keepdims=True)
    return x * lax.rsqrt(ms + EPS) * gain


def _sigmoid(x):
    return 1.0 / (1.0 + jnp.exp(-x))


def _window_sum(ext, w, rows):
    e = ext
    span = 1
    while span < w:
        n = e.shape[0]
        e = e[span:n, :] + e[0:n - span, :]
        span *= 2
    start = POOL_HALO + 1 - w
    return e[start:start + rows, :]


def _even_in_kernel(seq_len, h_ref, g_ref, w_ref, pw_ref, ps_ref, qkg_ref, bd_ref, o_ref,
                    xn_ref, carry_ref):
    i = pl.program_id(0)
    j = pl.program_id(1)
    tm = h_ref.shape[0]

    @pl.when(j == 0)
    def _():
        xn_ref[...] = _rms_rows(h_ref[...], g_ref[...]).astype(BF16)

    acc = _dot(xn_ref[...], w_ref[...])

    @pl.when(j == 0)
    def _():
        pos0 = (i * tm) % seq_len

        @pl.when(pos0 == 0)
        def _():
            carry_ref[...] = jnp.zeros_like(carry_ref)

        ext = jnp.concatenate([carry_ref[...], acc], axis=0)
        pos = pos0 + lax.broadcasted_iota(I32, (tm, 1), 0)
        outs = []
        for gi, w in enumerate(POOL_WINDOWS):
            sl = slice(gi * LANES, (gi + 1) * LANES)
            ws = _window_sum(ext[:, sl], w, tm)
            cnt = jnp.minimum(pos + 1, w).astype(F32)
            mixed = ws / cnt - acc[:, sl]
            outs.append(_dot(mixed.astype(BF16), pw_ref[gi]))
        o_ref[...] = jnp.concatenate(outs, axis=1) * ps_ref[...]
        carry_ref[...] = acc[tm - POOL_HALO:tm, :]

    def qk_norm(row):
        ms = _dot_hilo(acc * acc, bd_ref[...])
        o_ref[...] = acc * lax.rsqrt(ms + EPS) * qkg_ref[row:row + 1, :]

    @pl.when(j == 1)
    def _():
        qk_norm(0)

    @pl.when(j == 2)
    def _():
        qk_norm(1)

    @pl.when(j == 3)
    def _():
        o_ref[...] = acc


def _even_in(h2, gain, w_in, pool_w, pool_scale, qk_gain, bd, seq_len):
    T, D = h2.shape
    N = w_in.shape[1]
    tm, tn = ROW_TILE, COL_TILE
    return pl.pallas_call(
        functools.partial(_even_in_kernel, seq_len),
        grid=(T // tm, N // tn),
        in_specs=[
            pl.BlockSpec((tm, D), lambda i, j: (i, 0)),
            pl.BlockSpec((1, D), lambda i, j: (0, 0)),
            pl.BlockSpec((D, tn), lambda i, j: (0, j)),
            pl.BlockSpec(pool_w.shape, lambda i, j: (0, 0, 0)),
            pl.BlockSpec((1, tn), lambda i, j: (0, 0)),
            pl.BlockSpec((2, tn), lambda i, j: (0, 0)),
            pl.BlockSpec((tn, tn), lambda i, j: (0, 0)),
        ],
        out_specs=pl.BlockSpec((tm, tn), lambda i, j: (i, j)),
        out_shape=jax.ShapeDtypeStruct((T, N), F32),
        scratch_shapes=[pltpu.VMEM((tm, D), BF16), pltpu.VMEM((POOL_HALO, tn), F32)],
        compiler_params=_params("arbitrary", "arbitrary"),
        name="even_in",
    )(h2, gain, w_in, pool_w, pool_scale, qk_gain, bd)


def _attn_kernel(dilation, q_ref, kp_ref, kc_ref, vp_ref, vc_ref, o_ref, lse_ref):
    n = pl.program_id(2)
    qb = ATTN_BLOCK
    qi = lax.broadcasted_iota(I32, (qb, qb), 0)
    kj = lax.broadcasted_iota(I32, (qb, qb), 1)
    lane = lax.broadcasted_iota(I32, (qb, LANES), 1)
    valid_c = kj <= qi
    valid_p = (kj >= qi) & (n > 0)
    dist_c = ((qi - kj) * dilation).astype(F32)
    dist_p = ((qi + qb - kj) * dilation).astype(F32)
    scale = ATTN_HEAD_DIM ** -0.5
    heads_per_tile = LANES // ATTN_HEAD_DIM
    for hp in range(ATTN_HEADS // heads_per_tile):
        sl = slice(hp * LANES, (hp + 1) * LANES)
        qp = q_ref[:, sl]
        kc = kc_ref[:, sl].astype(BF16)
        kp = kp_ref[:, sl].astype(BF16)
        vc = vc_ref[:, sl].astype(BF16)
        vp = vp_ref[:, sl].astype(BF16)
        o_tile = jnp.zeros((qb, LANES), F32)
        lse_tile = jnp.zeros((qb, LANES), F32)
        for e in range(heads_per_tile):
            head = hp * heads_per_tile + e
            slope = 2.0 ** (-8.0 * (head + 1) / ATTN_HEADS)
            in_head = (lane >= e * ATTN_HEAD_DIM) & (lane < (e + 1) * ATTN_HEAD_DIM)
            qm = jnp.where(in_head, qp, 0.0).astype(BF16)
            s_c = _dot_nt(qm, kc) * scale - slope * dist_c
            s_p = _dot_nt(qm, kp) * scale - slope * dist_p
            s_c = jnp.where(valid_c, s_c, NEG_INF)
            s_p = jnp.where(valid_p, s_p, NEG_INF)
            m = jnp.maximum(jnp.max(s_c, axis=-1, keepdims=True),
                            jnp.max(s_p, axis=-1, keepdims=True))
            p_c = jnp.exp(s_c - m)
            p_p = jnp.exp(s_p - m)
            l = jnp.sum(p_c, axis=-1, keepdims=True) + jnp.sum(p_p, axis=-1, keepdims=True)
            o = (_dot(p_c.astype(BF16), vc) + _dot(p_p.astype(BF16), vp)) / l
            lse = m + jnp.log(l)
            o_tile = jnp.where(in_head, o, o_tile)
            lse_tile = jnp.where(in_head, lse, lse_tile)
        o_ref[:, sl] = o_tile
        lse_ref[:, sl] = lse_tile


def _attn_branch(proj, dilation):
    B, S, W = proj.shape
    aw = ATTN_HEADS * ATTN_HEAD_DIM
    L = S // dilation
    nb = L // ATTN_BLOCK
    cols = W // aw
    pv = proj.reshape(B, L, dilation * W)
    blk = (None, ATTN_BLOCK, aw)

    def spec(col, prev):
        if prev:
            return pl.BlockSpec(blk, lambda b, r, n: (b, jnp.maximum(n - 1, 0), r * cols + col))
        return pl.BlockSpec(blk, lambda b, r, n: (b, n, r * cols + col))

    out_spec = pl.BlockSpec(blk, lambda b, r, n: (b, n, r))
    o, lse = pl.pallas_call(
        functools.partial(_attn_kernel, dilation),
        grid=(B, dilation, nb),
        in_specs=[spec(1, False), spec(2, True), spec(2, False), spec(3, True), spec(3, False)],
        out_specs=[out_spec, out_spec],
        out_shape=[jax.ShapeDtypeStruct((B, L, dilation * aw), F32)] * 2,
        compiler_params=_params("arbitrary", "arbitrary", "arbitrary"),
        name="attn",
    )(pv, pv, pv, pv, pv)
    return o.reshape(B, S, aw), lse.reshape(B, S, aw)


def _even_out_kernel(a_ref, o1, o2, o3, l1, l2, l3, wa_ref, wo_ref, h_ref, out_ref, mix_ref):
    j = pl.program_id(1)
    half = a_ref.shape[1]

    @pl.when(j == 0)
    def _():
        la, lb, lc = l1[...], l2[...], l3[...]
        mx = jnp.maximum(jnp.maximum(la, lb), lc)
        wa, wb, wc = jnp.exp(la - mx), jnp.exp(lb - mx), jnp.exp(lc - mx)
        den = wa + wb + wc
        o = (wa / den) * o1[...] + (wb / den) * o2[...] + (wc / den) * o3[...]
        mix_ref[:, 0:half] = a_ref[...].astype(BF16)
        mix_ref[:, half:2 * half] = o.astype(BF16)

    out_ref[...] = (h_ref[...] + _dot(mix_ref[:, 0:half], wa_ref[...])
                    + _dot(mix_ref[:, half:2 * half], wo_ref[...]))


def _even_out(proj2, os_, lses, w_out, h2):
    T, D = h2.shape
    half = w_out.shape[0] // 2
    tm, tn = ROW_TILE, COL_TILE
    row = pl.BlockSpec((tm, half), lambda i, j: (i, 0))
    return pl.pallas_call(
        _even_out_kernel,
        grid=(T // tm, D // tn),
        in_specs=[row] * 7 + [
            pl.BlockSpec((half, tn), lambda i, j: (0, j)),
            pl.BlockSpec((half, tn), lambda i, j: (1, j)),
            pl.BlockSpec((tm, tn), lambda i, j: (i, j)),
        ],
        out_specs=pl.BlockSpec((tm, tn), lambda i, j: (i, j)),
        out_shape=jax.ShapeDtypeStruct((T, D), F32),
        scratch_shapes=[pltpu.VMEM((tm, 2 * half), BF16)],
        compiler_params=_params("arbitrary", "arbitrary"),
        name="even_out",
    )(proj2, *os_, *lses, w_out, w_out, h2)


def _ssm_in_kernel(seq_len, n_plain, h_ref, g_ref, w_ref, cw_ref, cb_ref, o_ref, xn_ref, carry_ref):
    i = pl.program_id(0)
    j = pl.program_id(1)
    tm = h_ref.shape[0]

    @pl.when(j == 0)
    def _():
        xn_ref[...] = _rms_rows(h_ref[...], g_ref[...]).astype(BF16)

    acc = _dot(xn_ref[...], w_ref[...])

    @pl.when(j < n_plain)
    def _():
        o_ref[...] = acc

    @pl.when(j >= n_plain)
    def _():
        jc = j - n_plain
        pos0 = (i * tm) % seq_len

        @pl.when(pos0 == 0)
        def _():
            carry_ref[jc] = jnp.zeros(carry_ref.shape[1:], F32)

        ext = jnp.concatenate([carry_ref[jc], acc], axis=0)
        y = cb_ref[...]
        for k in range(SSM_CONV):
            off = CONV_HALO - (SSM_CONV - 1) + k
            y = y + cw_ref[k:k + 1, :] * ext[off:off + tm, :]
        o_ref[...] = y * _sigmoid(y)
        carry_ref[jc] = acc[tm - CONV_HALO:tm, :]


def _ssm_in(h2, gain, w_main, conv_w, conv_b, seq_len, inner):
    T, D = h2.shape
    N = w_main.shape[1]
    tm, tn = ROW_TILE, COL_TILE
    n_plain = inner // tn
    n_conv = N // tn - n_plain
    return pl.pallas_call(
        functools.partial(_ssm_in_kernel, seq_len, n_plain),
        grid=(T // tm, N // tn),
        in_specs=[
            pl.BlockSpec((tm, D), lambda i, j: (i, 0)),
            pl.BlockSpec((1, D), lambda i, j: (0, 0)),
            pl.BlockSpec((D, tn), lambda i, j: (0, j)),
            pl.BlockSpec((SSM_CONV, tn), lambda i, j: (0, jnp.maximum(j - n_plain, 0))),
            pl.BlockSpec((1, tn), lambda i, j: (0, jnp.maximum(j - n_plain, 0))),
        ],
        out_specs=pl.BlockSpec((tm, tn), lambda i, j: (i, j)),
        out_shape=jax.ShapeDtypeStruct((T, N), F32),
        scratch_shapes=[pltpu.VMEM((tm, D), BF16), pltpu.VMEM((n_conv, CONV_HALO, tn), F32)],
        compiler_params=_params("arbitrary", "arbitrary"),
        name="ssm_in",
    )(h2, gain, w_main, conv_w, conv_b)


def _dt_kernel(h_ref, g_ref, w_ref, b_ref, o_ref):
    xn = _rms_rows(h_ref[...], g_ref[...]).astype(BF16)
    raw = _dot(xn, w_ref[...]) + b_ref[...]
    o_ref[...] = jnp.maximum(raw, 0.0) + jnp.log(1.0 + jnp.exp(-jnp.abs(raw)))


def _dt_proj(h2, gain, w_dt, dt_bias):
    T, D = h2.shape
    tm = ROW_TILE
    return pl.pallas_call(
        _dt_kernel,
        grid=(T // tm,),
        in_specs=[
            pl.BlockSpec((tm, D), lambda i: (i, 0)),
            pl.BlockSpec((1, D), lambda i: (0, 0)),
            pl.BlockSpec((D, LANES), lambda i: (0, 0)),
            pl.BlockSpec((1, LANES), lambda i: (0, 0)),
        ],
        out_specs=pl.BlockSpec((tm, LANES), lambda i: (i, 0)),
        out_shape=jax.ShapeDtypeStruct((T, LANES), F32),
        compiler_params=_params("arbitrary"),
        name="dt",
    )(h2, gain, w_dt, dt_bias)


def _ssd_kernel(z_ref, x_ref, b_ref, c_ref, dt_ref, a_ref, dsk_ref, gg_ref, ex_ref, o_ref, st_ref):
    c = pl.program_id(1)
    q = SSM_CHUNK
    gw = x_ref.shape[1] // SSM_GROUPS
    hpg = gw // SSM_HEAD_DIM

    @pl.when(c == 0)
    def _():
        st_ref[...] = jnp.zeros_like(st_ref)

    li = lax.broadcasted_iota(I32, (q, q), 0)
    si = lax.broadcasted_iota(I32, (q, q), 1)
    causal = si <= li
    tril = jnp.where(causal, 1.0, 0.0).astype(BF16)

    dt = dt_ref[...]
    da = dt * a_ref[...]
    acum = _dot_hilo_rhs(tril, da)
    acum_t = acum.T
    ex = ex_ref[...]
    dt_x = _dot_hilo(dt, ex)
    acum_x = _dot_hilo(acum, ex)
    last_x = acum_x[q - 1:q, :]
    x = x_ref[...]
    xdt = x * dt_x
    grow = jnp.exp(acum_x)
    xw = (xdt * jnp.exp(last_x - acum_x)).astype(BF16)
    sdec = jnp.exp(last_x)
    xdt_b = xdt.astype(BF16)

    for g in range(SSM_GROUPS):
        gs = slice(g * gw, (g + 1) * gw)
        bg = b_ref[:, g * SSM_STATE:(g + 1) * SSM_STATE]
        cg = c_ref[:, g * SSM_STATE:(g + 1) * SSM_STATE].astype(BF16)
        cb = _dot_nt(cg, bg.astype(BF16))
        bg_t = bg.T.astype(BF16)
        yd = []
        for r in range(hpg):
            hh = g * hpg + r
            seg = acum[:, hh:hh + 1] - acum_t[hh:hh + 1, :]
            decay = jnp.exp(jnp.where(causal, seg, NEG_INF))
            m = (cb * decay).astype(BF16)
            yd.append(_dot(m, xdt_b[:, g * gw + r * SSM_HEAD_DIM:g * gw + (r + 1) * SSM_HEAD_DIM]))
        y = jnp.concatenate(yd, axis=1)
        st = st_ref[g]
        y = y + _dot(cg, st.astype(BF16)) * grow[:, gs]
        st_ref[g] = st * sdec[:, gs] + _dot(bg_t, xw[:, gs])
        y = y + dsk_ref[:, gs] * x[:, gs]
        zg = z_ref[:, gs]
        y = y * (zg * _sigmoid(zg))
        o_ref[:, gs] = _rms_rows(y, gg_ref[:, gs])


def _dot_hilo_rhs(a_bf16, b):
    hi = b.astype(BF16)
    lo = (b - hi.astype(F32)).astype(BF16)
    return _dot(a_bf16, hi) + _dot(a_bf16, lo)


def _ssd(zxbc, dt, a_row, dskip_x, gate_gain, expand, batch, inner):
    T = zxbc.shape[0]
    q = SSM_CHUNK
    nc = T // batch // q
    bw = SSM_GROUPS * SSM_STATE
    row = lambda col: (lambda b, c: (b * nc + c, col))
    full = lambda b, c: (0, 0)
    return pl.pallas_call(
        _ssd_kernel,
        grid=(batch, nc),
        in_specs=[
            pl.BlockSpec((q, inner), row(0)),
            pl.BlockSpec((q, inner), row(1)),
            pl.BlockSpec((q, bw), row(2 * inner // bw)),
            pl.BlockSpec((q, bw), row(2 * inner // bw + 1)),
            pl.BlockSpec((q, LANES), row(0)),
            pl.BlockSpec((1, LANES), full),
            pl.BlockSpec((1, inner), full),
            pl.BlockSpec((1, inner), full),
            pl.BlockSpec((LANES, inner), full),
        ],
        out_specs=pl.BlockSpec((q, inner), row(0)),
        out_shape=jax.ShapeDtypeStruct((T, inner), F32),
        scratch_shapes=[pltpu.VMEM((SSM_GROUPS, SSM_STATE, inner // SSM_GROUPS), F32)],
        compiler_params=_params("arbitrary", "arbitrary"),
        name="ssd",
    )(zxbc, zxbc, zxbc, zxbc, dt, a_row, dskip_x, gate_gain, expand)


def _proj_res_kernel(x_ref, w_ref, h_ref, o_ref):
    o_ref[...] = h_ref[...] + _dot(x_ref[...].astype(BF16), w_ref[...])


def _proj_res(x2, w, h2):
    T, K = x2.shape
    D = w.shape[1]
    tm, tn = ROW_TILE, COL_TILE
    return pl.pallas_call(
        _proj_res_kernel,
        grid=(T // tm, D // tn),
        in_specs=[
            pl.BlockSpec((tm, K), lambda i, j: (i, 0)),
            pl.BlockSpec((K, tn), lambda i, j: (0, j)),
            pl.BlockSpec((tm, tn), lambda i, j: (i, j)),
        ],
        out_specs=pl.BlockSpec((tm, tn), lambda i, j: (i, j)),
        out_shape=jax.ShapeDtypeStruct((T, D), F32),
        compiler_params=_params("arbitrary", "arbitrary"),
        name="proj_res",
    )(x2, w, h2)


def _peer_score_kernel(h_ref, g_ref, w_ref, k_ref, s_ref, xn_out_ref, xn_ref):
    j = pl.program_id(1)

    @pl.when(j == 0)
    def _():
        xn = _rms_rows(h_ref[...], g_ref[...])
        xn_out_ref[...] = xn
        xn_ref[...] = xn.astype(BF16)

    qv = _dot(xn_ref[...], w_ref[...])
    for cidx in range(qv.shape[1] // LANES):
        qc = qv[:, cidx * LANES:(cidx + 1) * LANES]
        hi = qc.astype(BF16)
        lo = (qc - hi.astype(F32)).astype(BF16)
        keys = k_ref[cidx % 2]
        s_ref[cidx] = _dot_nt(keys, hi) + _dot_nt(keys, lo)


def _peer_scores(h2, gain, w_q, keys):
    T, D = h2.shape
    N = w_q.shape[1]
    tm, tn = ROW_TILE, COL_TILE
    per = tn // LANES
    return pl.pallas_call(
        _peer_score_kernel,
        grid=(T // tm, N // tn),
        in_specs=[
            pl.BlockSpec((tm, D), lambda i, j: (i, 0)),
            pl.BlockSpec((1, D), lambda i, j: (0, 0)),
            pl.BlockSpec((D, tn), lambda i, j: (0, j)),
            pl.BlockSpec(keys.shape, lambda i, j: (0, 0, 0)),
        ],
        out_specs=[pl.BlockSpec((per, PEER_N_KEYS, tm), lambda i, j: (j, 0, i)),
                   pl.BlockSpec((tm, D), lambda i, j: (i, 0))],
        out_shape=[jax.ShapeDtypeStruct((N // LANES, PEER_N_KEYS, T), F32),
                   jax.ShapeDtypeStruct((T, D), F32)],
        scratch_shapes=[pltpu.VMEM((tm, D), BF16)],
        compiler_params=_params("arbitrary", "arbitrary"),
        name="peer_scores",
    )(h2, gain, w_q, keys)


PAIR_COUNTS = tuple(PEER_TOPK // (a + 1) for a in range(PEER_TOPK))
N_PAIRS = sum(PAIR_COUNTS)
PAIR_ROWS = -(-N_PAIRS // SUBLANES) * SUBLANES


def _first_max(v, rows):
    m = jnp.max(v, axis=0, keepdims=True)
    idx = jnp.min(jnp.where(v == m, rows, float(v.shape[0])), axis=0, keepdims=True)
    return m, idx


def _peer_topk_kernel(s_ref, idx_ref, gate_ref, s_scr, m_scr, i_scr, cs_scr, ci_scr,
                      bs_scr, be_scr):
    h = pl.program_id(1)
    tt = s_ref.shape[2]
    k = PEER_TOPK
    rows_k = lax.broadcasted_iota(I32, (PEER_N_KEYS, tt), 0).astype(F32)
    rows_p = lax.broadcasted_iota(I32, (PAIR_ROWS, tt), 0).astype(F32)

    s_scr[...] = s_ref[...]

    def stage1(a, carry):
        for half in range(2):
            v = s_scr[half]
            m, idx = _first_max(v, rows_k)
            s_scr[half] = jnp.where(rows_k == idx, NEG_INF, v)
            m_scr[half, pl.ds(a, 1), :] = m
            i_scr[half, pl.ds(a, 1), :] = idx
        return carry

    lax.fori_loop(0, k, stage1, 0)

    off = 0
    for a, nb in enumerate(PAIR_COUNTS):
        cs_scr[off:off + nb, :] = m_scr[0, a:a + 1, :] + m_scr[1, 0:nb, :]
        ci_scr[off:off + nb, :] = i_scr[0, a:a + 1, :] * float(PEER_N_KEYS) + i_scr[1, 0:nb, :]
        off += nb
    cs_scr[N_PAIRS:PAIR_ROWS, :] = jnp.full((PAIR_ROWS - N_PAIRS, tt), NEG_INF, F32)
    ci_scr[N_PAIRS:PAIR_ROWS, :] = jnp.zeros((PAIR_ROWS - N_PAIRS, tt), F32)
    ci = ci_scr[...]

    def stage2(n, carry):
        cs = cs_scr[...]
        m, pos = _first_max(cs, rows_p)
        hit = rows_p == pos
        cs_scr[...] = jnp.where(hit, NEG_INF, cs)
        bs_scr[pl.ds(h * k + n, 1), :] = m
        be_scr[pl.ds(h * k + n, 1), :] = jnp.max(jnp.where(hit, ci, -1.0), axis=0, keepdims=True)
        return carry

    lax.fori_loop(0, k, stage2, 0)

    @pl.when(h == PEER_HEADS - 1)
    def _():
        gates = []
        for hh in range(PEER_HEADS):
            blk = bs_scr[hh * k:(hh + 1) * k, :]
            p = jnp.exp(blk - jnp.max(blk, axis=0, keepdims=True))
            gates.append(p / jnp.sum(p, axis=0, keepdims=True))
        gate_ref[...] = jnp.concatenate(gates, axis=0).T
        idx_ref[...] = (be_scr[...].T * float(WORD_ROWS)).astype(I32)


def _peer_topk(scores_t):
    n_chunks, n_keys, T = scores_t.shape
    tt = 1024
    return pl.pallas_call(
        _peer_topk_kernel,
        grid=(T // tt, PEER_HEADS),
        in_specs=[pl.BlockSpec((2, n_keys, tt), lambda i, h: (h, 0, i))],
        out_specs=[pl.BlockSpec((tt, PEER_SEL), lambda i, h: (i, 0)),
                   pl.BlockSpec((tt, PEER_SEL), lambda i, h: (i, 0))],
        out_shape=[jax.ShapeDtypeStruct((T, PEER_SEL), I32),
                   jax.ShapeDtypeStruct((T, PEER_SEL), F32)],
        scratch_shapes=[pltpu.VMEM((2, n_keys, tt), F32),
                        pltpu.VMEM((2, PEER_TOPK, tt), F32), pltpu.VMEM((2, PEER_TOPK, tt), F32),
                        pltpu.VMEM((PAIR_ROWS, tt), F32), pltpu.VMEM((PAIR_ROWS, tt), F32),
                        pltpu.VMEM((PEER_SEL, tt), F32), pltpu.VMEM((PEER_SEL, tt), F32)],
        compiler_params=_params("arbitrary", "arbitrary"),
        name="peer_topk",
    )(scores_t)


def _gather_experts(idx_ref, t, tab_ref, stage_ref):
    for kk in range(PEER_SEL):
        r0 = pl.multiple_of(idx_ref[t, kk], WORD_ROWS)
        stage_ref[kk * WORD_ROWS:(kk + 1) * WORD_ROWS, :] = tab_ref[pl.ds(r0, WORD_ROWS), :]


def _staged(stage_ref):
    return pltpu.bitcast(stage_ref[...], BF16)


def _idx_copy(idx_hbm, idx_smem, sems, half_block, slot):
    start = pl.multiple_of(half_block * IDX_HALF, IDX_HALF)
    return pltpu.make_async_copy(idx_hbm.at[pl.ds(start, IDX_HALF)], idx_smem.at[slot], sems.at[slot])


def _for_each_token(idx_hbm, idx_smem, sems, tab_ref, stage_a, stage_b, compute):
    i = pl.program_id(0)
    n = pl.num_programs(0)
    stages = (stage_a, stage_b)

    @pl.when(i == 0)
    def _():
        _idx_copy(idx_hbm, idx_smem, sems, 0, 0).start()

    n_tok = 2 * IDX_HALF
    _idx_copy(idx_hbm, idx_smem, sems, 2 * i, 0).wait()
    _idx_copy(idx_hbm, idx_smem, sems, 2 * i + 1, 1).start()
    _gather_experts(idx_smem.at[0], 0, tab_ref, stages[0])
    for t in range(n_tok):
        if t + 1 == IDX_HALF:
            _idx_copy(idx_hbm, idx_smem, sems, 2 * i + 1, 1).wait()

            @pl.when(i + 1 < n)
            def _():
                _idx_copy(idx_hbm, idx_smem, sems, 2 * i + 2, 0).start()

        if t + 1 < n_tok:
            nxt = t + 1
            _gather_experts(idx_smem.at[nxt // IDX_HALF], nxt % IDX_HALF, tab_ref, stages[nxt % 2])
        compute(t, _staged(stages[t % 2]))


def _diag_mask():
    n = PEER_SEL * SUBLANES
    row = lax.broadcasted_iota(I32, (SUBLANES, n), 0)
    col = lax.broadcasted_iota(I32, (SUBLANES, n), 1)
    return (col % SUBLANES == row).astype(F32)


def _peer_down_kernel(idx_hbm, x_ref, gate_ref, tab_ref, sel_ref, coef_ref, stage_a, stage_b, r_ref,
                      idx_smem, sems):
    mask = _diag_mask()

    def compute(t, w):
        x_tile = x_ref[pl.ds(t, 1), :].reshape(SUBLANES, LANES)
        y = _dot_nt(x_tile.astype(BF16), w)
        r_ref[pl.ds(t, 1), :] = jnp.sum(y * mask, axis=0, keepdims=True)

    _for_each_token(idx_hbm, idx_smem, sems, tab_ref, stage_a, stage_b, compute)
    act = _dot_hilo(r_ref[...], sel_ref[...])
    gelu = 0.5 * act * (1.0 + lax.erf(act * (2.0 ** -0.5)))
    coef_ref[...] = gate_ref[...] * gelu


def _peer_down(idx, xn, gate, tab, sel):
    T = idx.shape[0]
    tb = 2 * IDX_HALF
    n = PEER_SEL * SUBLANES
    return pl.pallas_call(
        _peer_down_kernel,
        grid=(T // tb,),
        in_specs=[
            pl.BlockSpec(memory_space=pl.ANY),
            pl.BlockSpec((tb, n), lambda i: (i, 0)),
            pl.BlockSpec((tb, PEER_SEL), lambda i: (i, 0)),
            pl.BlockSpec(memory_space=pltpu.VMEM),
            pl.BlockSpec(memory_space=pltpu.VMEM),
        ],
        out_specs=pl.BlockSpec((tb, PEER_SEL), lambda i: (i, 0)),
        out_shape=jax.ShapeDtypeStruct((T, PEER_SEL), F32),
        scratch_shapes=[pltpu.VMEM((PEER_SEL * WORD_ROWS, LANES), jnp.uint32), pltpu.VMEM((PEER_SEL * WORD_ROWS, LANES), jnp.uint32),
                        pltpu.VMEM((tb, n), F32), pltpu.SMEM((2, IDX_HALF, PEER_SEL), I32), pltpu.SemaphoreType.DMA((2,))],
        compiler_params=_params("arbitrary"),
        name="peer_down",
    )(idx, xn, gate, tab, sel)


def _peer_up_kernel(idx_hbm, coef_ref, h_ref, tab_ref, rep_ref, o_ref, stage_a, stage_b, crep_ref,
                    idx_smem, sems):
    mask = _diag_mask()
    crep_ref[...] = _dot(coef_ref[...].astype(BF16), rep_ref[...])

    def compute(t, w):
        cm = (crep_ref[pl.ds(t, 1), :] * mask).astype(BF16)
        o_tile = h_ref[pl.ds(t, 1), :].reshape(SUBLANES, LANES) + _dot(cm, w)
        o_ref[pl.ds(t, 1), :] = o_tile.reshape(1, SUBLANES * LANES)

    _for_each_token(idx_hbm, idx_smem, sems, tab_ref, stage_a, stage_b, compute)


def _peer_up(idx, coef, h2, tab, rep):
    T = idx.shape[0]
    tb = 2 * IDX_HALF
    n = PEER_SEL * SUBLANES
    return pl.pallas_call(
        _peer_up_kernel,
        grid=(T // tb,),
        in_specs=[
            pl.BlockSpec(memory_space=pl.ANY),
            pl.BlockSpec((tb, PEER_SEL), lambda i: (i, 0)),
            pl.BlockSpec((tb, n), lambda i: (i, 0)),
            pl.BlockSpec(memory_space=pltpu.VMEM),
            pl.BlockSpec(memory_space=pltpu.VMEM),
        ],
        out_specs=pl.BlockSpec((tb, n), lambda i: (i, 0)),
        out_shape=jax.ShapeDtypeStruct((T, n), F32),
        scratch_shapes=[pltpu.VMEM((PEER_SEL * WORD_ROWS, LANES), jnp.uint32), pltpu.VMEM((PEER_SEL * WORD_ROWS, LANES), jnp.uint32),
                        pltpu.VMEM((tb, n), F32),
                        pltpu.SMEM((2, IDX_HALF, PEER_SEL), I32), pltpu.SemaphoreType.DMA((2,))],
        compiler_params=_params("arbitrary"),
        name="peer_up",
    )(idx, coef, h2, tab, rep)


def _pack_table(tab):
    e, d = tab.shape
    t = tab.astype(BF16).reshape(e, WORD_ROWS, 2, LANES)
    t = jnp.swapaxes(t, 2, 3)
    return lax.bitcast_convert_type(t, jnp.uint32).reshape(e * WORD_ROWS, LANES)


def _peer_layer(h2, gain, w_q, sub_keys, tab_u, tab_v):
    T, D = h2.shape
    n_rep = PEER_SEL * SUBLANES
    rep_lane = jnp.arange(n_rep)
    rep = (jnp.arange(PEER_SEL)[:, None] == rep_lane[None, :] // SUBLANES).astype(BF16)
    scores, xn = _peer_scores(h2, gain.reshape(1, D), w_q.astype(BF16), sub_keys.astype(BF16))
    idx, gate = _peer_topk(scores)
    coef = _peer_down(idx, xn, gate, _pack_table(tab_u), rep.T)
    return _peer_up(idx, coef, h2, _pack_table(tab_v), rep)


def _even_layer(h2, gain, w_in, pool_w, pool_scale, q_gain, k_gain, w_out, batch, seq_len):
    T, D = h2.shape
    aw = ATTN_HEADS * ATTN_HEAD_DIM
    lane = jnp.arange(aw)
    bd = ((lane[:, None] // ATTN_HEAD_DIM == lane[None, :] // ATTN_HEAD_DIM)
          .astype(F32) / ATTN_HEAD_DIM).astype(BF16)
    qk_gain = jnp.stack([jnp.tile(q_gain, ATTN_HEADS), jnp.tile(k_gain, ATTN_HEADS)])
    proj = _even_in(h2, gain, w_in.astype(BF16), pool_w.astype(BF16),
                    pool_scale.reshape(1, -1), qk_gain, bd, seq_len)
    proj3 = proj.reshape(batch, seq_len, -1)
    outs, lses = [], []
    for _, dilation in DILATED_BRANCHES:
        o, lse = _attn_branch(proj3, dilation)
        outs.append(o.reshape(T, aw))
        lses.append(lse.reshape(T, aw))
    return _even_out(proj, outs, lses, w_out.astype(BF16), h2)


def _ssm_layer(h2, gain, w_in, conv_w, conv_b, dt_bias, a_log, d_skip, gate_gain, w_out,
               batch, seq_len):
    inner = w_out.shape[0]
    n_heads = inner // SSM_HEAD_DIM
    main_w = 2 * inner + 2 * SSM_GROUPS * SSM_STATE
    zxbc = _ssm_in(h2, gain, w_in[:, :main_w].astype(BF16), conv_w, conv_b.reshape(1, -1),
                   seq_len, inner)
    w_dt = jnp.pad(w_in[:, main_w:], ((0, 0), (0, LANES - n_heads))).astype(BF16)
    b_dt = jnp.pad(dt_bias, (0, LANES - n_heads)).reshape(1, LANES)
    dt = _dt_proj(h2, gain, w_dt, b_dt)
    a_row = jnp.pad(-jnp.exp(a_log), (0, LANES - n_heads)).reshape(1, LANES)
    dskip_x = jnp.repeat(d_skip, SSM_HEAD_DIM).reshape(1, inner)
    expand = (jnp.arange(LANES)[:, None] == jnp.arange(inner)[None, :] // SSM_HEAD_DIM).astype(BF16)
    y = _ssd(zxbc, dt, a_row, dskip_x, gate_gain.reshape(1, inner), expand, batch, inner)
    return _proj_res(y, w_out.astype(BF16), h2)


def kernel(x, attn_norm, ffn_norm, even_w_in, pool_w, pool_scale, q_gain, k_gain, even_w_out, ssm_w_in, conv_w, conv_b, dt_bias, a_log, d_skip, gate_gain, ssm_w_out, peer_w_q, peer_sub_keys, peer_u, peer_v):
    B, S, D = x.shape
    T = B * S
    assert D == SUBLANES * LANES and S % ROW_TILE == 0
    assert all(S % (d * ATTN_BLOCK) == 0 for _, d in DILATED_BRANCHES)
    assert all(w // d == ATTN_BLOCK for w, d in DILATED_BRANCHES)
    h = x.reshape(T, D)
    for layer in range(attn_norm.shape[0]):
        i = layer // 2
        gain = attn_norm[layer].reshape(1, D)
        if layer % 2 == 0:
            h = _even_layer(h, gain, even_w_in[i], pool_w[i], pool_scale[i], q_gain[i], k_gain[i],
                            even_w_out[i], B, S)
        else:
            h = _ssm_layer(h, gain, ssm_w_in[i], conv_w[i], conv_b[i], dt_bias[i], a_log[i],
                           d_skip[i], gate_gain[i], ssm_w_out[i], B, S)
        h = _peer_layer(h, ffn_norm[layer], peer_w_q[layer], peer_sub_keys[layer],
                        peer_u[layer], peer_v[layer])
    return h.reshape(B, S, D)
```
